```python
import math, functools
import jax, jax.numpy as jnp
from jax import lax
import numpy as np

D_MODEL = 1024
BATCH = 2
SEQ = 8192
DEPTH = 2
DEC_BATCH = 128
DEC_SEQ = 1
PAST_LEN = 2048
PAGE_SIZE = 128

N_ATT_HEADS = 4
D_HEAD_QK = 64
D_HEAD_V = 2 * D_HEAD_QK
D_QK = N_ATT_HEADS * 2 * D_HEAD_QK
D_ATT = N_ATT_HEADS * D_HEAD_V
D_LRU = D_MODEL - D_ATT
N_LRU_BLOCKS = 8
D_LRU_BLOCK = D_LRU // N_LRU_BLOCKS
CONV_WIDTH = 4
LRU_C = 8.0
D_IN = 2 * D_QK + D_ATT + 2 * D_LRU
SPLITS = (D_QK, 2 * D_QK, 2 * D_QK + D_ATT, 2 * D_QK + D_ATT + D_LRU)
D_FF = ((8 * D_MODEL + 3 * 256 - 1) // (3 * 256)) * 256
Q_BLOCK = 128
NORM_EPS = 1e-6

kernel_name = 'hymba_diffattn_rglru_adaln_step'


def _rmsnorm(x, g):
    x32 = x.astype(jnp.float32)
    y = x32 * lax.rsqrt(jnp.mean(x32 * x32, axis=-1, keepdims=True) + NORM_EPS)
    return (y * g.astype(jnp.float32)).astype(x.dtype)


def _alibi_slopes():
    return jnp.asarray([2.0 ** (-8.0 * (h + 1) / N_ATT_HEADS) for h in range(N_ATT_HEADS)], jnp.float32)


def _diff_lambda(lq1, lk1, lq2, lk2, lam_init):
    f = lambda a: a.astype(jnp.float32)
    return jnp.exp(jnp.sum(f(lq1) * f(lk1))) - jnp.exp(jnp.sum(f(lq2) * f(lk2))) + lam_init


def _diff_attend(q, k, v, q_pos, k_pos, lam):
    s = jnp.einsum('bqhcd,bkhcd->bhcqk', q, k, preferred_element_type=jnp.float32) * (D_HEAD_QK ** -0.5)
    dist = (q_pos[:, None] - k_pos[None, :]).astype(jnp.float32)
    bias = -_alibi_slopes()[:, None, None] * dist
    s = jnp.where(dist >= 0, s + bias[:, None], -jnp.inf)
    p = jax.nn.softmax(s, axis=-1)
    p = p[:, :, 0] - lam * p[:, :, 1]
    return jnp.einsum('bhqk,bkhd->bqhd', p.astype(v.dtype), v)


def _attend_prompt(q, k, v, lam):
    b, s = q.shape[0], q.shape[1]
    nb = s // Q_BLOCK
    qb = q.reshape(b, nb, Q_BLOCK, N_ATT_HEADS, 2, D_HEAD_QK).swapaxes(0, 1)
    k_pos = jnp.arange(s, dtype=jnp.int32)

    def one_block(args):
        q_blk, i = args
        q_pos = i * Q_BLOCK + jnp.arange(Q_BLOCK, dtype=jnp.int32)
        return _diff_attend(q_blk, k, v, q_pos, k_pos, lam)

    o = lax.map(one_block, (qb, jnp.arange(nb, dtype=jnp.int32)))
    return o.swapaxes(0, 1).reshape(b, s, N_ATT_HEADS, D_HEAD_V)


def _attend_sample(q, k, v, k_pages, v_pages, page_table, lam):
    b, t = q.shape[0], q.shape[1]
    k_past = k_pages[page_table].astype(k.dtype).reshape(b, PAST_LEN, N_ATT_HEADS, 2, D_HEAD_QK)
    v_past = v_pages[page_table].astype(v.dtype).reshape(b, PAST_LEN, N_ATT_HEADS, D_HEAD_V)
    k_all = jnp.concatenate([k_past, k], axis=1)
    v_all = jnp.concatenate([v_past, v], axis=1)
    q_pos = PAST_LEN + jnp.arange(t, dtype=jnp.int32)
    k_pos = jnp.arange(PAST_LEN + t, dtype=jnp.int32)
    return _diff_attend(q, k_all, v_all, q_pos, k_pos, lam)


def _rglru_branch(xr, gr, conv_buf, h0, conv_w, conv_b, lru_wa, lru_ba, lru_wx, lru_bx, lru_lambda):
    b, t = xr.shape[0], xr.shape[1]
    xp = jnp.concatenate([conv_buf.astype(xr.dtype), xr], axis=1)
    u = conv_b
    for j in range(CONV_WIDTH):
        u = u + xp[:, j:j + t] * conv_w[j]
    conv_new = xp[:, -(CONV_WIDTH - 1):]
    ub = u.reshape(b, t, N_LRU_BLOCKS, D_LRU_BLOCK)
    r = jax.nn.sigmoid(jnp.einsum('btni,nij->btnj', ub, lru_wa) + lru_ba).reshape(b, t, D_LRU)
    i_g = jax.nn.sigmoid(jnp.einsum('btni,nij->btnj', ub, lru_wx) + lru_bx).reshape(b, t, D_LRU)
    log_a = -LRU_C * r.astype(jnp.float32) * jax.nn.softplus(-lru_lambda.astype(jnp.float32))
    a = jnp.exp(log_a)
    bt = jnp.sqrt(-jnp.expm1(2.0 * log_a)) * (i_g * u).astype(jnp.float32)
    bt = bt.at[:, 0].add(a[:, 0] * h0.astype(jnp.float32))

    def comb(e1, e2):
        a1, b1 = e1
        a2, b2 = e2
        return a1 * a2, a2 * b1 + b2

    _, hs = lax.associative_scan(comb, (a, bt), axis=1)
    y = hs.astype(xr.dtype) * jax.nn.gelu(gr)
    return y, conv_new, hs[:, -1].astype(h0.dtype)


def _layer(x, c, lw, lam_init, attend, conv_buf, h0):
    (w_ada, b_ada, g_pre_mix, w_in, g_subln, conv_w, conv_b, lru_wa, lru_ba, lru_wx, lru_bx,
     lru_lambda, w_out, g_post_mix, g_pre_ffn, w_ffn_in, w_ffn_out, g_post_ffn) = lw
    b, t = x.shape[0], x.shape[1]
    mod = jnp.einsum('bd,de->be', jax.nn.silu(c), w_ada) + b_ada
    sh_m, sc_m, gt_m, sh_f, sc_f, gt_f = [m[:, None, :] for m in jnp.split(mod, 6, axis=-1)]
    h = _rmsnorm(x, g_pre_mix) * (1.0 + sc_m) + sh_m
    z = jnp.einsum('btd,de->bte', h, w_in)
    q, k, v, xr, gr = jnp.split(z, SPLITS, axis=-1)
    q = q.reshape(b, t, N_ATT_HEADS, 2, D_HEAD_QK)
    k = k.reshape(b, t, N_ATT_HEADS, 2, D_HEAD_QK)
    v = v.reshape(b, t, N_ATT_HEADS, D_HEAD_V)
    o_att = _rmsnorm(attend(q, k, v), g_subln) * (1.0 - lam_init)
    y_lru, conv_new, h_new = _rglru_branch(xr, gr, conv_buf, h0, conv_w, conv_b,
                                           lru_wa, lru_ba, lru_wx, lru_bx, lru_lambda)
    mix = jnp.concatenate([o_att.reshape(b, t, D_ATT), y_lru], axis=-1)
    mix = jnp.einsum('bte,ed->btd', mix, w_out)
    x = x + gt_m * _rmsnorm(mix, g_post_mix)
    h = _rmsnorm(x, g_pre_ffn) * (1.0 + sc_f) + sh_f
    gate, up = jnp.split(jnp.einsum('btd,df->btf', h, w_ffn_in), 2, axis=-1)
    f = jnp.einsum('btf,fd->btd', jax.nn.silu(gate) * up, w_ffn_out)
    x = x + gt_f * _rmsnorm(f, g_post_ffn)
    return x, k.reshape(b, t, N_ATT_HEADS, 2 * D_HEAD_QK), v, h_new, conv_new


def setup_inputs(seed: int = 0) -> dict:
    key = jax.random.key(seed)
    ks = iter(jax.random.split(key, 48))
    nrm = lambda shape, scale: jax.random.normal(next(ks), shape, jnp.float32) * scale
    gain = lambda shape: 1.0 + nrm(shape, 0.02)
    n_pages = PAST_LEN // PAGE_SIZE
    n_used = DEC_BATCH * n_pages
    n_pool = n_used + (n_used + 3) // 4
    page_table = jax.random.permutation(next(ks), n_pool)[:n_used].reshape(DEC_BATCH, n_pages).astype(jnp.int32)
    u = jax.random.uniform(next(ks), (DEPTH, D_LRU), jnp.float32, minval=0.9, maxval=0.999)
    a_base = u ** (1.0 / LRU_C)
    lru_lambda = jnp.log(a_base) - jnp.log1p(-a_base)
    return {
        'x_prompt': nrm((BATCH, SEQ, D_MODEL), 1.0),
        'x_sample': nrm((DEC_BATCH, DEC_SEQ, D_MODEL), 1.0),
        'cache_k': nrm((DEPTH, n_pool, PAGE_SIZE, N_ATT_HEADS, 2 * D_HEAD_QK), 1.0),
        'cache_v': nrm((DEPTH, n_pool, PAGE_SIZE, N_ATT_HEADS, D_HEAD_V), 1.0),
        'state_h': nrm((DEPTH, DEC_BATCH, D_LRU), 0.5),
        'state_conv': nrm((DEPTH, DEC_BATCH, CONV_WIDTH - 1, D_LRU), 1.0),
        'page_table': page_table,
        'c_prompt': nrm((BATCH, D_MODEL), 1.0),
        'c_sample': nrm((DEC_BATCH, D_MODEL), 1.0),
        'w_ada': nrm((DEPTH, D_MODEL, 6 * D_MODEL), 0.5 * D_MODEL ** -0.5),
        'b_ada': nrm((DEPTH, 6 * D_MODEL), 0.01),
        'g_pre_mix': gain((DEPTH, D_MODEL)),
        'w_in': nrm((DEPTH, D_MODEL, D_IN), D_MODEL ** -0.5),
        'lam_q1': nrm((DEPTH, D_HEAD_QK), 0.1),
        'lam_k1': nrm((DEPTH, D_HEAD_QK), 0.1),
        'lam_q2': nrm((DEPTH, D_HEAD_QK), 0.1),
        'lam_k2': nrm((DEPTH, D_HEAD_QK), 0.1),
        'g_subln': gain((DEPTH, D_HEAD_V)),
        'conv_w': nrm((DEPTH, CONV_WIDTH, D_LRU), CONV_WIDTH ** -0.5),
        'conv_b': nrm((DEPTH, D_LRU), 0.01),
        'lru_wa': nrm((DEPTH, N_LRU_BLOCKS, D_LRU_BLOCK, D_LRU_BLOCK), D_LRU_BLOCK ** -0.5),
        'lru_ba': nrm((DEPTH, N_LRU_BLOCKS, D_LRU_BLOCK), 0.01),
        'lru_wx': nrm((DEPTH, N_LRU_BLOCKS, D_LRU_BLOCK, D_LRU_BLOCK), D_LRU_BLOCK ** -0.5),
        'lru_bx': nrm((DEPTH, N_LRU_BLOCKS, D_LRU_BLOCK), 0.01),
        'lru_lambda': lru_lambda,
        'w_out': nrm((DEPTH, D_ATT + D_LRU, D_MODEL), (D_ATT + D_LRU) ** -0.5),
        'g_post_mix': gain((DEPTH, D_MODEL)),
        'g_pre_ffn': gain((DEPTH, D_MODEL)),
        'w_ffn_in': nrm((DEPTH, D_MODEL, 2 * D_FF), D_MODEL ** -0.5),
        'w_ffn_out': nrm((DEPTH, D_FF, D_MODEL), D_FF ** -0.5),
        'g_post_ffn': gain((DEPTH, D_MODEL)),
    }


def reference(x_prompt, x_sample, cache_k, cache_v, state_h, state_conv, page_table, c_prompt, c_sample,
              w_ada, b_ada, g_pre_mix, w_in, lam_q1, lam_k1, lam_q2, lam_k2, g_subln, conv_w, conv_b,
              lru_wa, lru_ba, lru_wx, lru_bx, lru_lambda, w_out, g_post_mix, g_pre_ffn, w_ffn_in,
              w_ffn_out, g_post_ffn):
    yp, ys = x_prompt, x_sample
    bp = x_prompt.shape[0]
    kp_l, vp_l, hp_l, cp_l, ks_l, vs_l, hs_l, cs_l = [], [], [], [], [], [], [], []
    for l in range(DEPTH):
        lam_init = 0.8 - 0.6 * math.exp(-0.3 * l)
        lam = _diff_lambda(lam_q1[l], lam_k1[l], lam_q2[l], lam_k2[l], lam_init)
        lw = (w_ada[l], b_ada[l], g_pre_mix[l], w_in[l], g_subln[l], conv_w[l], conv_b[l], lru_wa[l],
              lru_ba[l], lru_wx[l], lru_bx[l], lru_lambda[l], w_out[l], g_post_mix[l], g_pre_ffn[l],
              w_ffn_in[l], w_ffn_out[l], g_post_ffn[l])
        attend_p = functools.partial(_attend_prompt, lam=lam)
        conv0 = jnp.zeros((bp, CONV_WIDTH - 1, D_LRU), x_prompt.dtype)
        h0 = jnp.zeros((bp, D_LRU), x_prompt.dtype)
        yp, kp, vp, hp, cp = _layer(yp, c_prompt, lw, lam_init, attend_p, conv0, h0)
        attend_s = functools.partial(_attend_sample, k_pages=cache_k[l], v_pages=cache_v[l],
                                     page_table=page_table, lam=lam)
        ys, ks_, vs_, hs_, cs_ = _layer(ys, c_sample, lw, lam_init, attend_s, state_conv[l], state_h[l])
        kp_l.append(kp); vp_l.append(vp); hp_l.append(hp); cp_l.append(cp)
        ks_l.append(ks_); vs_l.append(vs_); hs_l.append(hs_); cs_l.append(cs_)
    k_prompt = jnp.stack(kp_l)
    v_prompt = jnp.stack(vp_l)
    h_prompt = jnp.stack(hp_l)
    conv_prompt = jnp.stack(cp_l)
    k_sample = jnp.stack(ks_l)
    v_sample = jnp.stack(vs_l)
    h_sample = jnp.stack(hs_l)
    conv_sample = jnp.stack(cs_l)
    return (yp, ys, k_prompt, v_prompt, h_prompt, conv_prompt, k_sample, v_sample, h_sample, conv_sample)
```

```python
import functools
import math

import numpy as np
import jax
import jax.numpy as jnp
from jax import lax
from jax.experimental import pallas as pl
from jax.experimental.pallas import tpu as pltpu

D_MODEL = 1024
N_HEADS = 4
D_QK = 64
D_V = 2 * D_QK
D_ATT = N_HEADS * D_V
D_LRU = D_MODEL - D_ATT
N_LRU_BLOCKS = 8
CONV_WIDTH = 4
LRU_C = 8.0
D_IN = 3 * D_ATT + 2 * D_LRU
NORM_EPS = 1e-6
QK_SCALE = D_QK ** -0.5

F32 = jnp.float32
BF16 = jnp.bfloat16

VMEM_LIMIT_BYTES = 56 * 1024 * 1024

TM_PROJ = 512
TM_FFN = 256
TQ = 512
T_LRU = 256
ADA_TN = 1536
SUBLANES = 8
BF16_ROWS = 16


def _cparams(n_axes):
    return pltpu.CompilerParams(
        dimension_semantics=("arbitrary",) * n_axes,
        vmem_limit_bytes=VMEM_LIMIT_BYTES)


def _rms(x, g):
    return x * lax.rsqrt(jnp.mean(x * x, axis=-1, keepdims=True) + NORM_EPS) * g


def _dot(a, b):
    return jnp.dot(a, b, preferred_element_type=F32)


def _dot_nt(a, b):
    return lax.dot_general(a, b, (((1,), (1,)), ((), ())), preferred_element_type=F32)


def _dot_tn(a, b):
    return lax.dot_general(a, b, (((0,), (0,)), ((), ())), preferred_element_type=F32)


def _ada_kernel(c_ref, w_ref, b_ref, o_ref):
    c = c_ref[...]
    h = (c * jax.nn.sigmoid(c)).astype(BF16)
    o_ref[...] = _dot(h, w_ref[...]) + b_ref[...]


def _ada_call(c_all, w_ada_bf, b_ada):
    depth, _, n = w_ada_bf.shape
    rows = c_all.shape[0]
    return pl.pallas_call(
        _ada_kernel,
        grid=(depth, n // ADA_TN),
        in_specs=[
            pl.BlockSpec((rows, D_MODEL), lambda l, j: (0, 0)),
            pl.BlockSpec((None, D_MODEL, ADA_TN), lambda l, j: (l, 0, j)),
            pl.BlockSpec((None, 1, ADA_TN), lambda l, j: (l, 0, j)),
        ],
        out_specs=pl.BlockSpec((None, rows, ADA_TN), lambda l, j: (l, 0, j)),
        out_shape=jax.ShapeDtypeStruct((depth, rows, n), F32),
        compiler_params=_cparams(2),
        name="ada_mod",
    )(c_all, w_ada_bf, b_ada.reshape(depth, 1, n))


def _inproj_kernel(x_ref, sh_ref, sc_ref, g_ref, w_ref,
                   qb_ref, k_ref, kb_ref, v_ref, vb_ref, xr_ref, gr_ref):
    x = x_ref[...]
    h = (_rms(x, g_ref[...]) * (1.0 + sc_ref[...]) + sh_ref[...]).astype(BF16)
    qb_ref[...] = (_dot(h, w_ref[:, 0:D_ATT]) * QK_SCALE).astype(BF16)
    k = _dot(h, w_ref[:, D_ATT:2 * D_ATT])
    k_ref[...] = k
    kb_ref[...] = k.astype(BF16)
    v = _dot(h, w_ref[:, 2 * D_ATT:3 * D_ATT])
    v_ref[...] = v
    vb_ref[...] = v.astype(BF16)
    xr_ref[...] = _dot(h, w_ref[:, 3 * D_ATT:3 * D_ATT + D_LRU])
    gr_ref[...] = _dot(h, w_ref[:, 3 * D_ATT + D_LRU:D_IN])


def _inproj_call(x, sh, sc, g, w_bf, tm, name):
    b, s, _ = x.shape
    r = sh.shape[1]
    rb = 1 if r == 1 else tm
    tok = lambda i, j: (i, j, 0)
    mod = (lambda i, j: (i, 0, 0)) if r == 1 else tok
    shapes = [
        jax.ShapeDtypeStruct((b, s, D_ATT), BF16),
        jax.ShapeDtypeStruct((b, s, D_ATT), F32),
        jax.ShapeDtypeStruct((b, s, D_ATT), BF16),
        jax.ShapeDtypeStruct((b, s, D_ATT), F32),
        jax.ShapeDtypeStruct((b, s, D_ATT), BF16),
        jax.ShapeDtypeStruct((b, s, D_LRU), F32),
        jax.ShapeDtypeStruct((b, s, D_LRU), F32),
    ]
    return pl.pallas_call(
        _inproj_kernel,
        grid=(b, s // tm),
        in_specs=[
            pl.BlockSpec((None, tm, D_MODEL), tok),
            pl.BlockSpec((None, rb, D_MODEL), mod),
            pl.BlockSpec((None, rb, D_MODEL), mod),
            pl.BlockSpec((1, D_MODEL), lambda i, j: (0, 0)),
            pl.BlockSpec((D_MODEL, D_IN), lambda i, j: (0, 0)),
        ],
        out_specs=[pl.BlockSpec((None, tm, D_ATT), tok) for _ in shapes],
        out_shape=shapes,
        compiler_params=_cparams(2),
        name=name,
    )(x, sh, sc, g, w_bf)


def _diff_lambda(lam_ref, lam_init):
    lv = lam_ref[...]
    s1 = jnp.sum(lv[0:1] * lv[1:2], axis=-1, keepdims=True)
    s2 = jnp.sum(lv[2:3] * lv[3:4], axis=-1, keepdims=True)
    return jnp.exp(s1) - jnp.exp(s2) + lam_init


def _head_norm(o, g, lam_init):
    return _rms(o, g) * (1.0 - lam_init)


def _attn_kernel(slope_ref, q_ref, k_ref, v_ref, lam_ref, g_ref, o_ref,
                 m_sc, l_sc, acc_sc, *, lam_init):
    h = pl.program_id(1)
    qi = pl.program_id(2)
    slope = slope_ref[h]

    q = q_ref[...]
    lane = lax.broadcasted_iota(jnp.int32, q.shape, 1)
    zero = jnp.zeros_like(q)
    qq = jnp.concatenate([jnp.where(lane < D_QK, q, zero),
                          jnp.where(lane >= D_QK, q, zero)], axis=0)

    row = lax.broadcasted_iota(jnp.int32, (2 * TQ, TQ), 0) & (TQ - 1)
    col = lax.broadcasted_iota(jnp.int32, (2 * TQ, TQ), 1)
    rel2 = row - col
    rel_bias2 = slope * rel2.astype(F32)
    causal2 = rel2 >= 0

    m_sc[...] = jnp.full(m_sc.shape, -jnp.inf, F32)
    l_sc[...] = jnp.zeros(l_sc.shape, F32)
    acc_sc[...] = jnp.zeros(acc_sc.shape, F32)

    def block(kj, masked):
        start = pl.multiple_of(kj * TQ, TQ)
        kb = k_ref[pl.ds(start, TQ), :]
        vb = v_ref[pl.ds(start, TQ), :]
        s = _dot_nt(qq, kb) - rel_bias2
        if masked:
            s = jnp.where(causal2, s, -jnp.inf)
        off = slope * ((qi - kj) * TQ).astype(F32)
        m_prev = m_sc[...]
        m_new = jnp.maximum(m_prev, jnp.max(s, axis=-1, keepdims=True) - off)
        p = jnp.exp(s - (m_new + off))
        alpha = jnp.exp(m_prev - m_new)
        l_sc[...] = alpha * l_sc[...] + jnp.sum(p, axis=-1, keepdims=True)
        acc_sc[...] = alpha * acc_sc[...] + _dot(p.astype(BF16), vb)
        m_sc[...] = m_new

    def body(kj, carry):
        block(kj, masked=False)
        return carry

    lax.fori_loop(0, qi, body, 0)
    block(qi, masked=True)

    lam = _diff_lambda(lam_ref, lam_init)
    o_all = acc_sc[...] / l_sc[...]
    o = o_all[:TQ] - lam * o_all[TQ:]
    o_ref[...] = _head_norm(o, g_ref[...], lam_init).astype(o_ref.dtype)


def _attn_call(slopes, q_bf, k_bf, v_bf, lamv, g_subln, lam_init, name):
    b, s, _ = q_bf.shape
    qmap = lambda i, h, j: (i, j, h)
    kvmap = lambda i, h, j: (i, 0, h)
    const = lambda i, h, j: (0, 0)
    return pl.pallas_call(
        functools.partial(_attn_kernel, lam_init=lam_init),
        grid=(b, N_HEADS, s // TQ),
        in_specs=[
            pl.BlockSpec(memory_space=pltpu.SMEM),
            pl.BlockSpec((None, TQ, D_V), qmap),
            pl.BlockSpec((None, s, D_V), kvmap),
            pl.BlockSpec((None, s, D_V), kvmap),
            pl.BlockSpec((4, D_QK), const),
            pl.BlockSpec((1, D_V), const),
        ],
        out_specs=pl.BlockSpec((None, TQ, D_V), qmap),
        out_shape=jax.ShapeDtypeStruct((b, s, D_ATT), BF16),
        scratch_shapes=[
            pltpu.VMEM((2 * TQ, 1), F32),
            pltpu.VMEM((2 * TQ, 1), F32),
            pltpu.VMEM((2 * TQ, D_V), F32),
        ],
        compiler_params=_cparams(3),
        name=name,
    )(slopes, q_bf, k_bf, v_bf, lamv, g_subln)


def _decode_attn_kernel(pt_ref, q_ref, kn_ref, vn_ref, bias_ref, lam_ref, g_ref, *rest,
                        n_pages, lam_init):
    del pt_ref
    k_pages = rest[:n_pages]
    v_pages = rest[n_pages:2 * n_pages]
    o_ref = rest[2 * n_pages]

    q = q_ref[...]
    lanes = D_V
    r = lax.broadcasted_iota(jnp.int32, (lanes, D_ATT), 0)
    c = lax.broadcasted_iota(jnp.int32, (lanes, D_ATT), 1)
    col = ((c >> 6) & 1) * N_HEADS + (c >> 7)
    qt = jnp.where(r == col, jnp.broadcast_to(q, (lanes, D_ATT)), 0.0).astype(BF16)

    s = jnp.concatenate([_dot_nt(kp[...].astype(BF16), qt) for kp in k_pages], axis=0)
    s = s + bias_ref[...]

    rown = lax.broadcasted_iota(jnp.int32, (BF16_ROWS, D_ATT), 0)
    kn = jnp.where(rown == 0, jnp.broadcast_to(kn_ref[...], (BF16_ROWS, D_ATT)), 0.0)
    vn = jnp.where(rown == 0, jnp.broadcast_to(vn_ref[...], (BF16_ROWS, D_ATT)), 0.0)
    first = lax.broadcasted_iota(jnp.int32, (BF16_ROWS, lanes), 0) == 0
    s_new = jnp.where(first, _dot_nt(kn.astype(BF16), qt), -jnp.inf)

    m = jnp.maximum(jnp.max(s, axis=0, keepdims=True), jnp.max(s_new, axis=0, keepdims=True))
    e = jnp.exp(s - m)
    e_new = jnp.exp(s_new - m)
    l = jnp.sum(e, axis=0, keepdims=True) + jnp.sum(e_new, axis=0, keepdims=True)
    p = (e / l).astype(BF16)
    p_new = (e_new / l).astype(BF16)

    page = k_pages[0].shape[0]
    acc = _dot_tn(p_new, vn.astype(BF16))
    for i, vp in enumerate(v_pages):
        acc = acc + _dot_tn(p[i * page:(i + 1) * page], vp[...].astype(BF16))

    lam = _diff_lambda(lam_ref, lam_init)
    g = g_ref[...]
    outs = []
    for h in range(N_HEADS):
        o1 = acc[h:h + 1, h * D_V:(h + 1) * D_V]
        o2 = acc[N_HEADS + h:N_HEADS + h + 1, h * D_V:(h + 1) * D_V]
        outs.append(_head_norm(o1 - lam * o2, g, lam_init))
    o_ref[...] = jnp.concatenate(outs, axis=-1).astype(o_ref.dtype)


def _decode_attn_call(page_table, q, k_new, v_new, bias, lamv, g_subln, cache_k, cache_v,
                      layer, lam_init, name):
    b = q.shape[0]
    n_pages = page_table.shape[1]
    page = cache_k.shape[2]
    tok = lambda i, pt: (i, 0, 0)
    const = lambda i, pt: (0, 0)

    def page_spec(j):
        return pl.BlockSpec((None, None, page, D_ATT),
                            lambda i, pt, j=j: (layer, pt[i, j], 0, 0))

    grid_spec = pltpu.PrefetchScalarGridSpec(
        num_scalar_prefetch=1,
        grid=(b,),
        in_specs=[
            pl.BlockSpec((None, 1, D_ATT), tok),
            pl.BlockSpec((None, 1, D_ATT), tok),
            pl.BlockSpec((None, 1, D_ATT), tok),
            pl.BlockSpec(bias.shape, const),
            pl.BlockSpec((4, D_QK), const),
            pl.BlockSpec((1, D_V), const),
        ] + [page_spec(j) for j in range(n_pages)] + [page_spec(j) for j in range(n_pages)],
        out_specs=pl.BlockSpec((None, 1, D_ATT), tok),
    )
    return pl.pallas_call(
        functools.partial(_decode_attn_kernel, n_pages=n_pages, lam_init=lam_init),
        grid_spec=grid_spec,
        out_shape=jax.ShapeDtypeStruct((b, 1, D_ATT), BF16),
        compiler_params=_cparams(1),
        name=name,
    )(page_table, q, k_new, v_new, bias, lamv, g_subln,
      *([cache_k] * n_pages), *([cache_v] * n_pages))


def _lru_gates(u, wg_ref, bg_ref, lam_ref):
    gates = jax.nn.sigmoid(_dot(u.astype(BF16), wg_ref[...]) + bg_ref[...])
    r = gates[:, :D_LRU]
    i_g = gates[:, D_LRU:]
    neg_lam = -lam_ref[...]
    softplus = jnp.maximum(neg_lam, 0.0) + jnp.log1p(jnp.exp(-jnp.abs(neg_lam)))
    log_a = -LRU_C * r * softplus
    a = jnp.exp(log_a)
    one_minus_a2 = -jnp.tanh(log_a) * (a * a + 1.0)
    return a, jnp.sqrt(one_minus_a2) * (i_g * u)


def _lru_seq_kernel(xr_ref, gr_ref, cw_ref, cb_ref, wg_ref, bg_ref, lam_ref,
                    y_ref, h_ref, tail_ref, xbuf, hbuf):
    t = T_LRU
    j = pl.program_id(1)

    @pl.when(j == 0)
    def _():
        xbuf[0:SUBLANES, :] = jnp.zeros((SUBLANES, D_LRU), F32)
        hbuf[...] = jnp.zeros(hbuf.shape, F32)

    x = xr_ref[...]
    xbuf[SUBLANES:SUBLANES + t, :] = x
    cw = cw_ref[...]
    u = cb_ref[...] + cw[CONV_WIDTH - 1:CONV_WIDTH] * x
    for d in range(1, CONV_WIDTH):
        u = u + cw[CONV_WIDTH - 1 - d:CONV_WIDTH - d] * xbuf[SUBLANES - d:SUBLANES - d + t, :]

    a, b = _lru_gates(u, wg_ref, bg_ref, lam_ref)

    row = lax.broadcasted_iota(jnp.int32, (t, D_LRU), 0)
    shift = 1
    while shift < t:
        a_prev = pltpu.roll(a, shift, 0)
        b_prev = pltpu.roll(b, shift, 0)
        live = row >= shift
        b = jnp.where(live, a * b_prev + b, b)
        a = jnp.where(live, a * a_prev, a)
        shift *= 2

    h0 = hbuf[SUBLANES - 1:SUBLANES, :]
    hs = b + a * h0
    y_ref[...] = (hs * jax.nn.gelu(gr_ref[...])).astype(y_ref.dtype)

    hbuf[...] = hs[t - SUBLANES:t]
    xbuf[0:SUBLANES, :] = x[t - SUBLANES:t]
    h_ref[...] = hs[t - SUBLANES:t]
    tail_ref[...] = x[t - SUBLANES:t]


def _lru_seq_call(xr, gr, conv_w, conv_b, wg_bf, bg, lam, name):
    b, s, _ = xr.shape
    tok = lambda i, j: (i, j, 0)
    const = lambda i, j: (0, 0)
    per_seq = lambda i, j: (i, 0, 0)
    return pl.pallas_call(
        _lru_seq_kernel,
        grid=(b, s // T_LRU),
        in_specs=[
            pl.BlockSpec((None, T_LRU, D_LRU), tok),
            pl.BlockSpec((None, T_LRU, D_LRU), tok),
            pl.BlockSpec((CONV_WIDTH, D_LRU), const),
            pl.BlockSpec((1, D_LRU), const),
            pl.BlockSpec((D_LRU, 2 * D_LRU), const),
            pl.BlockSpec((1, 2 * D_LRU), const),
            pl.BlockSpec((1, D_LRU), const),
        ],
        out_specs=[
            pl.BlockSpec((None, T_LRU, D_LRU), tok),
            pl.BlockSpec((None, SUBLANES, D_LRU), per_seq),
            pl.BlockSpec((None, SUBLANES, D_LRU), per_seq),
        ],
        out_shape=[
            jax.ShapeDtypeStruct((b, s, D_LRU), BF16),
            jax.ShapeDtypeStruct((b, SUBLANES, D_LRU), F32),
            jax.ShapeDtypeStruct((b, SUBLANES, D_LRU), F32),
        ],
        scratch_shapes=[
            pltpu.VMEM((SUBLANES + T_LRU, D_LRU), F32),
            pltpu.VMEM((SUBLANES, D_LRU), F32),
        ],
        compiler_params=_cparams(2),
        name=name,
    )(xr, gr, conv_w, conv_b, wg_bf, bg, lam)


def _lru_step_kernel(xr_ref, gr_ref, cbuf_ref, h0_ref, cw_ref, cb_ref, wg_ref, bg_ref, lam_ref,
                     y_ref, h_ref):
    x = xr_ref[...]
    cw = cw_ref[...]
    u = cb_ref[...] + cw[CONV_WIDTH - 1:CONV_WIDTH] * x
    for d in range(CONV_WIDTH - 1):
        u = u + cw[d:d + 1] * cbuf_ref[d]
    a, b = _lru_gates(u, wg_ref, bg_ref, lam_ref)
    h = a * h0_ref[...] + b
    h_ref[...] = h
    y_ref[...] = (h * jax.nn.gelu(gr_ref[...])).astype(y_ref.dtype)


def _lru_step_call(xr, gr, cbuf_t, h0, conv_w, conv_b, wg_bf, bg, lam, name):
    b = xr.shape[0]
    return pl.pallas_call(
        _lru_step_kernel,
        out_shape=[jax.ShapeDtypeStruct((b, D_LRU), BF16),
                   jax.ShapeDtypeStruct((b, D_LRU), F32)],
        compiler_params=pltpu.CompilerParams(vmem_limit_bytes=VMEM_LIMIT_BYTES),
        name=name,
    )(xr, gr, cbuf_t, h0, conv_w, conv_b, wg_bf, bg, lam)


def _outffn_kernel(x_ref, oa_ref, yl_ref, gtm_ref, shf_ref, scf_ref, gtf_ref,
                   gpm_ref, gpf_ref, gqf_ref, wo_ref, wi_ref, wf_ref, o_ref):
    d_ff = wf_ref.shape[0]
    mix = _dot(oa_ref[...], wo_ref[0:D_ATT, :]) + _dot(yl_ref[...], wo_ref[D_ATT:D_MODEL, :])
    x1 = x_ref[...] + gtm_ref[...] * _rms(mix, gpm_ref[...])
    h = (_rms(x1, gpf_ref[...]) * (1.0 + scf_ref[...]) + shf_ref[...]).astype(BF16)
    gate = _dot(h, wi_ref[:, 0:d_ff])
    up = _dot(h, wi_ref[:, d_ff:2 * d_ff])
    act = (gate * jax.nn.sigmoid(gate) * up).astype(BF16)
    f = _dot(act, wf_ref[...])
    o_ref[...] = x1 + gtf_ref[...] * _rms(f, gqf_ref[...])


def _outffn_call(x, oa, yl, gtm, shf, scf, gtf, gpm, gpf, gqf, wo_bf, wi_bf, wf_bf, tm, name):
    b, s, _ = x.shape
    r = gtm.shape[1]
    rb = 1 if r == 1 else tm
    d_ff = wf_bf.shape[0]
    tok = lambda i, j: (i, j, 0)
    mod = (lambda i, j: (i, 0, 0)) if r == 1 else tok
    const = lambda i, j: (0, 0)
    resident = pl.Buffered(1)
    return pl.pallas_call(
        _outffn_kernel,
        grid=(b, s // tm),
        in_specs=[
            pl.BlockSpec((None, tm, D_MODEL), tok),
            pl.BlockSpec((None, tm, D_ATT), tok),
            pl.BlockSpec((None, tm, D_LRU), tok),
            pl.BlockSpec((None, rb, D_MODEL), mod),
            pl.BlockSpec((None, rb, D_MODEL), mod),
            pl.BlockSpec((None, rb, D_MODEL), mod),
            pl.BlockSpec((None, rb, D_MODEL), mod),
            pl.BlockSpec((1, D_MODEL), const),
            pl.BlockSpec((1, D_MODEL), const),
            pl.BlockSpec((1, D_MODEL), const),
            pl.BlockSpec((D_MODEL, D_MODEL), const, pipeline_mode=resident),
            pl.BlockSpec((D_MODEL, 2 * d_ff), const, pipeline_mode=resident),
            pl.BlockSpec((d_ff, D_MODEL), const, pipeline_mode=resident),
        ],
        out_specs=pl.BlockSpec((None, tm, D_MODEL), tok),
        out_shape=jax.ShapeDtypeStruct((b, s, D_MODEL), F32),
        compiler_params=_cparams(2),
        name=name,
    )(x, oa, yl, gtm, shf, scf, gtf, gpm, gpf, gqf, wo_bf, wi_bf, wf_bf)


def _block_diag(w):
    n, i, j = w.shape
    eye = jnp.eye(n, dtype=w.dtype)
    return (eye[:, None, :, None] * w[:, :, None, :]).reshape(n * i, n * j)


def kernel(x_prompt, x_sample, cache_k, cache_v, state_h, state_conv, page_table, c_prompt, c_sample,
           w_ada, b_ada, g_pre_mix, w_in, lam_q1, lam_k1, lam_q2, lam_k2, g_subln, conv_w, conv_b,
           lru_wa, lru_ba, lru_wx, lru_bx, lru_lambda, w_out, g_post_mix, g_pre_ffn, w_ffn_in,
           w_ffn_out, g_post_ffn):
    depth = w_ada.shape[0]
    bp, seq, _ = x_prompt.shape
    bs = x_sample.shape[0]
    n_pool, page = cache_k.shape[1], cache_k.shape[2]
    past = page_table.shape[1] * page

    pad = (-(bs + bp)) % SUBLANES
    c_all = jnp.concatenate([c_sample, c_prompt, jnp.zeros((pad, D_MODEL), F32)], axis=0)
    mod = _ada_call(c_all, w_ada.astype(BF16), b_ada)

    ys = x_sample.reshape(1, bs, D_MODEL)
    yp = x_prompt
    ck = cache_k.reshape(depth, n_pool, page, D_ATT)
    cv = cache_v.reshape(depth, n_pool, page, D_ATT)

    slopes = jnp.asarray([2.0 ** (-8.0 * (h + 1) / N_HEADS) for h in range(N_HEADS)], F32)
    dist = (past - np.arange(past, dtype=np.float32))[:, None]
    slope_cols = np.zeros((1, D_V), np.float32)
    for j in range(2 * N_HEADS):
        slope_cols[0, j] = 2.0 ** (-8.0 * (j % N_HEADS + 1) / N_HEADS)
    dec_bias = jnp.asarray(-dist * slope_cols)

    outs = {n: [] for n in ("kp", "vp", "hp", "cp", "ks", "vs", "hs", "cs")}
    for l in range(depth):
        lam_init = 0.8 - 0.6 * math.exp(-0.3 * l)
        lamv = jnp.stack([lam_q1[l], lam_k1[l], lam_q2[l], lam_k2[l]])
        g_sub = g_subln[l].reshape(1, D_V)
        w_in_bf = w_in[l].astype(BF16)
        wo_bf = w_out[l].astype(BF16)
        wi_bf = w_ffn_in[l].astype(BF16)
        wf_bf = w_ffn_out[l].astype(BF16)
        wg_bf = jnp.concatenate([_block_diag(lru_wa[l]), _block_diag(lru_wx[l])], axis=1).astype(BF16)
        bg = jnp.concatenate([lru_ba[l].reshape(1, D_LRU), lru_bx[l].reshape(1, D_LRU)], axis=1)
        lam_row = lru_lambda[l].reshape(1, D_LRU)
        cb_row = conv_b[l].reshape(1, D_LRU)
        row = lambda v: v[l].reshape(1, D_MODEL)

        mods = [mod[l, :, i * D_MODEL:(i + 1) * D_MODEL] for i in range(6)]
        mod_s = [m[:bs].reshape(1, bs, D_MODEL) for m in mods]
        mod_p = [m[bs:bs + bp].reshape(bp, 1, D_MODEL) for m in mods]

        qb, k, kb, v, vb, xr, gr = _inproj_call(yp, mod_p[0], mod_p[1], row(g_pre_mix), w_in_bf,
                                                TM_PROJ, f"inproj_p{l}")
        oa = _attn_call(slopes, qb, kb, vb, lamv, g_sub, lam_init, f"attn_p{l}")
        yl, h8, x8 = _lru_seq_call(xr, gr, conv_w[l], cb_row, wg_bf, bg, lam_row, f"lru_p{l}")
        yp = _outffn_call(yp, oa, yl, mod_p[2], mod_p[3], mod_p[4], mod_p[5],
                          row(g_post_mix), row(g_pre_ffn), row(g_post_ffn),
                          wo_bf, wi_bf, wf_bf, TM_FFN, f"outffn_p{l}")
        outs["kp"].append(k.reshape(bp, seq, N_HEADS, D_V))
        outs["vp"].append(v.reshape(bp, seq, N_HEADS, D_V))
        outs["hp"].append(h8[:, SUBLANES - 1])
        outs["cp"].append(x8[:, SUBLANES - (CONV_WIDTH - 1):])

        qb, k, kb, v, vb, xr, gr = _inproj_call(ys, mod_s[0], mod_s[1], row(g_pre_mix), w_in_bf,
                                                bs, f"inproj_s{l}")
        q32 = qb.astype(F32).reshape(bs, 1, D_ATT)
        oa = _decode_attn_call(page_table, q32, k.reshape(bs, 1, D_ATT), v.reshape(bs, 1, D_ATT),
                               dec_bias, lamv, g_sub, ck, cv, l, lam_init, f"attn_s{l}")
        yl, h_new = _lru_step_call(xr[0], gr[0], jnp.swapaxes(state_conv[l], 0, 1), state_h[l],
                                   conv_w[l], cb_row, wg_bf, bg, lam_row, f"lru_s{l}")
        ys = _outffn_call(ys, oa.reshape(1, bs, D_ATT), yl.reshape(1, bs, D_LRU),
                          mod_s[2], mod_s[3], mod_s[4], mod_s[5],
                          row(g_post_mix), row(g_pre_ffn), row(g_post_ffn),
                          wo_bf, wi_bf, wf_bf, bs, f"outffn_s{l}")
        outs["ks"].append(k.reshape(bs, 1, N_HEADS, D_V))
        outs["vs"].append(v.reshape(bs, 1, N_HEADS, D_V))
        outs["hs"].append(h_new)
        outs["cs"].append(jnp.concatenate([state_conv[l][:, 1:], xr[0][:, None, :]], axis=1))

    st = {n: jnp.stack(v) for n, v in outs.items()}
    return (yp, ys.reshape(bs, 1, D_MODEL), st["kp"], st["vp"], st["hp"], st["cp"],
            st["ks"], st["vs"], st["hs"], st["cs"])
```

```python
import functools
import math

import numpy as np
import jax
import jax.numpy as jnp
from jax import lax
from jax.experimental import pallas as pl
from jax.experimental.pallas import tpu as pltpu

D_MODEL = 1024
N_HEADS = 4
D_QK = 64
D_V = 2 * D_QK
D_ATT = N_HEADS * D_V
D_LRU = D_MODEL - D_ATT
N_LRU_BLOCKS = 8
CONV_WIDTH = 4
LRU_C = 8.0
D_IN = 3 * D_ATT + 2 * D_LRU
NORM_EPS = 1e-6
QK_SCALE = D_QK ** -0.5

F32 = jnp.float32
BF16 = jnp.bfloat16

VMEM_LIMIT_BYTES = 56 * 1024 * 1024

SUBLANES = 8
BF16_ROWS = 16
LANES = 128

TM_PROJ = 512
TM_FFN = 256
TQ = TM_PROJ
T_LRU = 256
ADA_TN = 1536
BF16_EXACT_INT = 256


def _cparams(n_axes):
    return pltpu.CompilerParams(
        dimension_semantics=("arbitrary",) * n_axes,
        vmem_limit_bytes=VMEM_LIMIT_BYTES)


def _rms(x, g):
    return x * lax.rsqrt(jnp.mean(x * x, axis=-1, keepdims=True) + NORM_EPS) * g


def _dot(a, b):
    return jnp.dot(a, b, preferred_element_type=F32)


def _dot_nt(a, b):
    return lax.dot_general(a, b, (((1,), (1,)), ((), ())), preferred_element_type=F32)


def _ada_kernel(c_ref, w_ref, b_ref, o_ref):
    c = c_ref[...]
    h = (c * jax.nn.sigmoid(c)).astype(BF16)
    o_ref[...] = _dot(h, w_ref[...]) + b_ref[...]


def _ada_call(c_all, w_ada_bf, b_ada):
    depth, _, n = w_ada_bf.shape
    rows = c_all.shape[0]
    return pl.pallas_call(
        _ada_kernel,
        grid=(depth, n // ADA_TN),
        in_specs=[
            pl.BlockSpec((rows, D_MODEL), lambda l, j: (0, 0)),
            pl.BlockSpec((None, D_MODEL, ADA_TN), lambda l, j: (l, 0, j)),
            pl.BlockSpec((None, 1, ADA_TN), lambda l, j: (l, 0, j)),
        ],
        out_specs=pl.BlockSpec((None, rows, ADA_TN), lambda l, j: (l, 0, j)),
        out_shape=jax.ShapeDtypeStruct((depth, rows, n), F32),
        compiler_params=_cparams(2),
        name="ada_mod",
    )(c_all, w_ada_bf, b_ada.reshape(depth, 1, n))


def _store_heads(ref, x):
    for h in range(N_HEADS):
        ref[:, h, :] = x[:, h * D_V:(h + 1) * D_V]


def _inproj_kernel(x_ref, sh_ref, sc_ref, g_ref, w_ref, *outs, prompt):
    x = x_ref[...]
    h = (_rms(x, g_ref[...]) * (1.0 + sc_ref[...]) + sh_ref[...]).astype(BF16)
    q = _dot(h, w_ref[:, 0:D_ATT]) * QK_SCALE
    k = _dot(h, w_ref[:, D_ATT:2 * D_ATT])
    v = _dot(h, w_ref[:, 2 * D_ATT:3 * D_ATT])
    if prompt:
        qt_ref, k4_ref, kb_ref, v4_ref, vt_ref, xr_ref, gr_ref = outs
        qt_ref[...] = q.T.astype(BF16)
        kb_ref[...] = k.astype(BF16)
        vt_ref[...] = v.T.astype(BF16)
    else:
        q4_ref, k4_ref, v4_ref, xr_ref, gr_ref = outs
        _store_heads(q4_ref, q)
    _store_heads(k4_ref, k)
    _store_heads(v4_ref, v)
    xr_ref[...] = _dot(h, w_ref[:, 3 * D_ATT:3 * D_ATT + D_LRU])
    gr_ref[...] = _dot(h, w_ref[:, 3 * D_ATT + D_LRU:D_IN])


def _inproj_call(x, sh, sc, g, w_bf, tm, prompt, name):
    b, s, _ = x.shape
    nt = s // tm
    r = sh.shape[1]
    rb = 1 if r == 1 else tm
    tok = lambda i, j: (i, j, 0)
    tok4 = lambda i, j: (i, j, 0, 0)
    mod = (lambda i, j: (i, 0, 0)) if r == 1 else tok
    heads = (jax.ShapeDtypeStruct((b, s, N_HEADS, D_V), F32),
             pl.BlockSpec((None, tm, N_HEADS, D_V), tok4))
    flat32 = (jax.ShapeDtypeStruct((b, s, D_LRU), F32), pl.BlockSpec((None, tm, D_LRU), tok))
    flat16 = (jax.ShapeDtypeStruct((b, s, D_ATT), BF16), pl.BlockSpec((None, tm, D_ATT), tok))
    tiles_t = (jax.ShapeDtypeStruct((b, nt, D_ATT, tm), BF16),
               pl.BlockSpec((None, None, D_ATT, tm), tok4))
    if prompt:
        outs = [tiles_t, heads, flat16, heads, tiles_t, flat32, flat32]
    else:
        outs = [heads, heads, heads, flat32, flat32]
    return pl.pallas_call(
        functools.partial(_inproj_kernel, prompt=prompt),
        grid=(b, nt),
        in_specs=[
            pl.BlockSpec((None, tm, D_MODEL), tok),
            pl.BlockSpec((None, rb, D_MODEL), mod),
            pl.BlockSpec((None, rb, D_MODEL), mod),
            pl.BlockSpec((1, D_MODEL), lambda i, j: (0, 0)),
            pl.BlockSpec((D_MODEL, D_IN), lambda i, j: (0, 0)),
        ],
        out_specs=[o[1] for o in outs],
        out_shape=[o[0] for o in outs],
        compiler_params=_cparams(2),
        name=name,
    )(x, sh, sc, g, w_bf)


def _diff_lambda(lam_ref, lam_init):
    lv = lam_ref[...]
    s1 = jnp.sum(lv[0:1] * lv[1:2], axis=-1, keepdims=True)
    s2 = jnp.sum(lv[2:3] * lv[3:4], axis=-1, keepdims=True)
    return jnp.exp(s1) - jnp.exp(s2) + lam_init


def _attn_kernel(slope_ref, qt_ref, k_ref, vt_ref, lam_ref, g_ref, o_ref,
                 m_sc, acc_sc, *, lam_init):
    h = pl.program_id(1)
    qi = pl.program_id(2)
    slope = slope_ref[h]
    tq = TQ

    qt = qt_ref[...]
    zq = jnp.zeros((D_QK, tq), BF16)
    feat = jnp.concatenate(
        [jnp.concatenate([qt[0:D_QK], zq], axis=0), jnp.concatenate([zq, qt[D_QK:D_V]], axis=0)],
        axis=1)
    r = lax.broadcasted_iota(jnp.int32, (LANES, 2 * tq), 0)
    c = lax.broadcasted_iota(jnp.int32, (LANES, 2 * tq), 1) & (tq - 1)
    c_lo = (c & (BF16_EXACT_INT - 1)).astype(F32)
    c_hi = (c & ~(BF16_EXACT_INT - 1)).astype(F32)
    q_terms = jnp.where(r < 2, 1.0, jnp.where(r == 2, -slope * c_lo, jnp.where(r == 3, -slope * c_hi, 0.0)))
    rhs = jnp.concatenate([feat, q_terms.astype(BF16)], axis=0)

    kr = lax.broadcasted_iota(jnp.int32, (tq, LANES), 0)
    kl = lax.broadcasted_iota(jnp.int32, (tq, LANES), 1)
    k_lo = (kr & (BF16_EXACT_INT - 1)).astype(F32)
    k_hi = (kr & ~(BF16_EXACT_INT - 1)).astype(F32)
    k_terms = jnp.where(kl == 0, slope * k_lo, jnp.where(kl == 1, slope * k_hi, jnp.where(kl < 4, 1.0, 0.0)))
    k_terms = k_terms.astype(BF16)

    ones = jnp.ones((BF16_ROWS, tq), BF16)

    m_sc[...] = jnp.full(m_sc.shape, -jnp.inf, F32)
    acc_sc[...] = jnp.zeros(acc_sc.shape, F32)

    def block(kj, masked):
        start = pl.multiple_of(kj * tq, tq)
        lhs = jnp.concatenate([k_ref[pl.ds(start, tq), :], k_terms], axis=1)
        s = _dot(lhs, rhs)
        if masked:
            key = lax.broadcasted_iota(jnp.int32, s.shape, 0)
            qry = lax.broadcasted_iota(jnp.int32, s.shape, 1) & (tq - 1)
            s = jnp.where(qry >= key, s, -jnp.inf)
        off = slope * ((qi - kj) * tq).astype(F32)
        m_prev = m_sc[...]
        m_new = jnp.maximum(m_prev, jnp.max(s, axis=0, keepdims=True) - off)
        p = jnp.exp(s - (m_new + off)).astype(BF16)
        alpha = jnp.exp(m_prev - m_new)
        vt = jnp.concatenate([vt_ref[kj], ones], axis=0)
        acc_sc[...] = alpha * acc_sc[...] + _dot(vt, p)
        m_sc[...] = m_new

    def body(kj, carry):
        block(kj, masked=False)
        return carry

    lax.fori_loop(0, qi, body, 0)
    block(qi, masked=True)

    lam = _diff_lambda(lam_ref, lam_init)
    acc = acc_sc[...]
    o1 = acc[0:D_V, 0:tq] / acc[D_V:D_V + 1, 0:tq]
    o2 = acc[0:D_V, tq:2 * tq] / acc[D_V:D_V + 1, tq:2 * tq]
    o = o1 - lam * o2
    o = o * lax.rsqrt(jnp.mean(o * o, axis=0, keepdims=True) + NORM_EPS)
    o_ref[...] = (o.T * g_ref[...] * (1.0 - lam_init)).astype(o_ref.dtype)


def _attn_call(slopes, qt, k_bf, vt, lamv, g_subln, lam_init, name):
    b, nt = qt.shape[0], qt.shape[1]
    s = nt * TQ
    const = lambda i, h, j: (0, 0)
    return pl.pallas_call(
        functools.partial(_attn_kernel, lam_init=lam_init),
        grid=(b, N_HEADS, nt),
        in_specs=[
            pl.BlockSpec(memory_space=pltpu.SMEM),
            pl.BlockSpec((None, None, D_V, TQ), lambda i, h, j: (i, j, h, 0)),
            pl.BlockSpec((None, s, D_V), lambda i, h, j: (i, 0, h)),
            pl.BlockSpec((None, nt, D_V, TQ), lambda i, h, j: (i, 0, h, 0)),
            pl.BlockSpec((4, D_QK), const),
            pl.BlockSpec((1, D_V), const),
        ],
        out_specs=pl.BlockSpec((None, TQ, D_V), lambda i, h, j: (i, j, h)),
        out_shape=jax.ShapeDtypeStruct((b, s, D_ATT), BF16),
        scratch_shapes=[
            pltpu.VMEM((1, 2 * TQ), F32),
            pltpu.VMEM((D_V + BF16_ROWS, 2 * TQ), F32),
        ],
        compiler_params=_cparams(3),
        name=name,
    )(slopes, qt, k_bf, vt, lamv, g_subln)


def _decode_attn_kernel(pt_ref, q_ref, kn_ref, vn_ref, bias_ref, lam_ref, g_ref, *rest,
                        n_pages, lam_init):
    del pt_ref
    k_pages = rest[:n_pages]
    v_pages = rest[n_pages:2 * n_pages]
    o_ref = rest[2 * n_pages]
    rows = BF16_ROWS

    r = lax.broadcasted_iota(jnp.int32, (rows, D_V), 0)
    lane = lax.broadcasted_iota(jnp.int32, (rows, D_V), 1)
    first = r == 0
    qts, kns, vns = [], [], []
    for h in range(N_HEADS):
        mine = ((r == h) & (lane < D_QK)) | ((r == N_HEADS + h) & (lane >= D_QK))
        qts.append(jnp.where(mine, jnp.broadcast_to(q_ref[h:h + 1, :], (rows, D_V)), 0.0).astype(BF16))
        kns.append(jnp.where(first, jnp.broadcast_to(kn_ref[h:h + 1, :], (rows, D_V)), 0.0).astype(BF16))
        vns.append(jnp.where(first, jnp.broadcast_to(vn_ref[h:h + 1, :], (rows, D_V)), 0.0).astype(BF16))

    def scores(keys):
        acc = _dot_nt(qts[0], keys[0])
        for h in range(1, N_HEADS):
            acc = acc + _dot_nt(qts[h], keys[h])
        return acc

    s = jnp.concatenate(
        [scores([kp[:, h, :].astype(BF16) for h in range(N_HEADS)]) for kp in k_pages], axis=1)
    s = s + bias_ref[...]
    s_new = scores(kns)
    s_new = jnp.where(lax.broadcasted_iota(jnp.int32, s_new.shape, 1) == 0, s_new, -jnp.inf)

    m = jnp.maximum(jnp.max(s, axis=1, keepdims=True), jnp.max(s_new, axis=1, keepdims=True))
    e = jnp.exp(s - m)
    e_new = jnp.exp(s_new - m)
    l = jnp.sum(e, axis=1, keepdims=True) + jnp.sum(e_new, axis=1, keepdims=True)
    p = (e / l).astype(BF16)
    p_new = (e_new / l).astype(BF16)

    lam = _diff_lambda(lam_ref, lam_init)
    page = k_pages[0].shape[0]
    for h in range(N_HEADS):
        acc = _dot(p_new, vns[h])
        for i, vp in enumerate(v_pages):
            acc = acc + _dot(p[:, i * page:(i + 1) * page], vp[:, h, :].astype(BF16))
        o = acc[h:h + 1] - lam * acc[N_HEADS + h:N_HEADS + h + 1]
        o_ref[:, h * D_V:(h + 1) * D_V] = (_rms(o, g_ref[...]) * (1.0 - lam_init)).astype(o_ref.dtype)


def _decode_attn_call(page_table, q4, k4, v4, bias, lamv, g_subln, cache_k, cache_v,
                      layer, lam_init, name):
    b = q4.shape[0]
    n_pages = page_table.shape[1]
    page = cache_k.shape[2]
    tok = lambda i, pt: (i, 0, 0)
    const = lambda i, pt: (0, 0)

    def page_spec(j):
        return pl.BlockSpec((None, None, page, N_HEADS, D_V),
                            lambda i, pt, j=j: (layer, pt[i, j], 0, 0, 0))

    grid_spec = pltpu.PrefetchScalarGridSpec(
        num_scalar_prefetch=1,
        grid=(b,),
        in_specs=[
            pl.BlockSpec((None, N_HEADS, D_V), tok),
            pl.BlockSpec((None, N_HEADS, D_V), tok),
            pl.BlockSpec((None, N_HEADS, D_V), tok),
            pl.BlockSpec(bias.shape, const),
            pl.BlockSpec((4, D_QK), const),
            pl.BlockSpec((1, D_V), const),
        ] + [page_spec(j) for j in range(n_pages)] + [page_spec(j) for j in range(n_pages)],
        out_specs=pl.BlockSpec((None, 1, D_ATT), tok),
    )
    return pl.pallas_call(
        functools.partial(_decode_attn_kernel, n_pages=n_pages, lam_init=lam_init),
        grid_spec=grid_spec,
        out_shape=jax.ShapeDtypeStruct((b, 1, D_ATT), BF16),
        compiler_params=_cparams(1),
        name=name,
    )(page_table, q4, k4, v4, bias, lamv, g_subln,
      *([cache_k] * n_pages), *([cache_v] * n_pages))


def _lru_gates(u, wg_ref, bg_ref, lam_ref):
    gates = jax.nn.sigmoid(_dot(u.astype(BF16), wg_ref[...]) + bg_ref[...])
    r = gates[:, :D_LRU]
    i_g = gates[:, D_LRU:]
    neg_lam = -lam_ref[...]
    softplus = jnp.maximum(neg_lam, 0.0) + jnp.log1p(jnp.exp(-jnp.abs(neg_lam)))
    log_a = -LRU_C * r * softplus
    a = jnp.exp(log_a)
    one_minus_a2 = -jnp.tanh(log_a) * (a * a + 1.0)
    return a, jnp.sqrt(one_minus_a2) * (i_g * u)


def _lru_seq_kernel(xr_ref, gr_ref, cw_ref, cb_ref, wg_ref, bg_ref, lam_ref,
                    y_ref, h_ref, tail_ref, xbuf, hbuf):
    t = T_LRU
    j = pl.program_id(1)

    @pl.when(j == 0)
    def _():
        xbuf[0:SUBLANES, :] = jnp.zeros((SUBLANES, D_LRU), F32)
        hbuf[...] = jnp.zeros(hbuf.shape, F32)

    x = xr_ref[...]
    xbuf[SUBLANES:SUBLANES + t, :] = x
    cw = cw_ref[...]
    u = cb_ref[...] + cw[CONV_WIDTH - 1:CONV_WIDTH] * x
    for d in range(1, CONV_WIDTH):
        u = u + cw[CONV_WIDTH - 1 - d:CONV_WIDTH - d] * xbuf[SUBLANES - d:SUBLANES - d + t, :]

    a, b = _lru_gates(u, wg_ref, bg_ref, lam_ref)

    row = lax.broadcasted_iota(jnp.int32, (t, D_LRU), 0)
    shift = 1
    while shift < t:
        a_prev = pltpu.roll(a, shift, 0)
        b_prev = pltpu.roll(b, shift, 0)
        live = row >= shift
        b = jnp.where(live, a * b_prev + b, b)
        a = jnp.where(live, a * a_prev, a)
        shift *= 2

    h0 = hbuf[SUBLANES - 1:SUBLANES, :]
    hs = b + a * h0
    y_ref[...] = (hs * jax.nn.gelu(gr_ref[...])).astype(y_ref.dtype)

    hbuf[...] = hs[t - SUBLANES:t]
    xbuf[0:SUBLANES, :] = x[t - SUBLANES:t]
    h_ref[...] = hs[t - SUBLANES:t]
    tail_ref[...] = x[t - SUBLANES:t]


def _lru_seq_call(xr, gr, conv_w, conv_b, wg_bf, bg, lam, name):
    b, s, _ = xr.shape
    tok = lambda i, j: (i, j, 0)
    const = lambda i, j: (0, 0)
    per_seq = lambda i, j: (i, 0, 0)
    return pl.pallas_call(
        _lru_seq_kernel,
        grid=(b, s // T_LRU),
        in_specs=[
            pl.BlockSpec((None, T_LRU, D_LRU), tok),
            pl.BlockSpec((None, T_LRU, D_LRU), tok),
            pl.BlockSpec((CONV_WIDTH, D_LRU), const),
            pl.BlockSpec((1, D_LRU), const),
            pl.BlockSpec((D_LRU, 2 * D_LRU), const),
            pl.BlockSpec((1, 2 * D_LRU), const),
            pl.BlockSpec((1, D_LRU), const),
        ],
        out_specs=[
            pl.BlockSpec((None, T_LRU, D_LRU), tok),
            pl.BlockSpec((None, SUBLANES, D_LRU), per_seq),
            pl.BlockSpec((None, SUBLANES, D_LRU), per_seq),
        ],
        out_shape=[
            jax.ShapeDtypeStruct((b, s, D_LRU), BF16),
            jax.ShapeDtypeStruct((b, SUBLANES, D_LRU), F32),
            jax.ShapeDtypeStruct((b, SUBLANES, D_LRU), F32),
        ],
        scratch_shapes=[
            pltpu.VMEM((SUBLANES + T_LRU, D_LRU), F32),
            pltpu.VMEM((SUBLANES, D_LRU), F32),
        ],
        compiler_params=_cparams(2),
        name=name,
    )(xr, gr, conv_w, conv_b, wg_bf, bg, lam)


def _lru_step_kernel(xr_ref, gr_ref, cbuf_ref, h0_ref, cw_ref, cb_ref, wg_ref, bg_ref, lam_ref,
                     y_ref, h_ref):
    x = xr_ref[...]
    cw = cw_ref[...]
    u = cb_ref[...] + cw[CONV_WIDTH - 1:CONV_WIDTH] * x
    for d in range(CONV_WIDTH - 1):
        u = u + cw[d:d + 1] * cbuf_ref[d]
    a, b = _lru_gates(u, wg_ref, bg_ref, lam_ref)
    h = a * h0_ref[...] + b
    h_ref[...] = h
    y_ref[...] = (h * jax.nn.gelu(gr_ref[...])).astype(y_ref.dtype)


def _lru_step_call(xr, gr, cbuf_t, h0, conv_w, conv_b, wg_bf, bg, lam, name):
    b = xr.shape[0]
    return pl.pallas_call(
        _lru_step_kernel,
        out_shape=[jax.ShapeDtypeStruct((b, D_LRU), BF16),
                   jax.ShapeDtypeStruct((b, D_LRU), F32)],
        compiler_params=pltpu.CompilerParams(vmem_limit_bytes=VMEM_LIMIT_BYTES),
        name=name,
    )(xr, gr, cbuf_t, h0, conv_w, conv_b, wg_bf, bg, lam)


def _outffn_kernel(x_ref, oa_ref, yl_ref, gtm_ref, shf_ref, scf_ref, gtf_ref,
                   gpm_ref, gpf_ref, gqf_ref, wo_ref, wi_ref, wf_ref, o_ref):
    d_ff = wf_ref.shape[0]
    mix = _dot(oa_ref[...], wo_ref[0:D_ATT, :]) + _dot(yl_ref[...], wo_ref[D_ATT:D_MODEL, :])
    x1 = x_ref[...] + gtm_ref[...] * _rms(mix, gpm_ref[...])
    h = (_rms(x1, gpf_ref[...]) * (1.0 + scf_ref[...]) + shf_ref[...]).astype(BF16)
    gate = _dot(h, wi_ref[:, 0:d_ff])
    up = _dot(h, wi_ref[:, d_ff:2 * d_ff])
    act = (gate * jax.nn.sigmoid(gate) * up).astype(BF16)
    f = _dot(act, wf_ref[...])
    o_ref[...] = x1 + gtf_ref[...] * _rms(f, gqf_ref[...])


def _outffn_call(x, oa, yl, gtm, shf, scf, gtf, gpm, gpf, gqf, wo_bf, wi_bf, wf_bf, tm, name):
    b, s, _ = x.shape
    r = gtm.shape[1]
    rb = 1 if r == 1 else tm
    d_ff = wf_bf.shape[0]
    tok = lambda i, j: (i, j, 0)
    mod = (lambda i, j: (i, 0, 0)) if r == 1 else tok
    const = lambda i, j: (0, 0)
    resident = pl.Buffered(1)
    return pl.pallas_call(
        _outffn_kernel,
        grid=(b, s // tm),
        in_specs=[
            pl.BlockSpec((None, tm, D_MODEL), tok),
            pl.BlockSpec((None, tm, D_ATT), tok),
            pl.BlockSpec((None, tm, D_LRU), tok),
            pl.BlockSpec((None, rb, D_MODEL), mod),
            pl.BlockSpec((None, rb, D_MODEL), mod),
            pl.BlockSpec((None, rb, D_MODEL), mod),
            pl.BlockSpec((None, rb, D_MODEL), mod),
            pl.BlockSpec((1, D_MODEL), const),
            pl.BlockSpec((1, D_MODEL), const),
            pl.BlockSpec((1, D_MODEL), const),
            pl.BlockSpec((D_MODEL, D_MODEL), const, pipeline_mode=resident),
            pl.BlockSpec((D_MODEL, 2 * d_ff), const, pipeline_mode=resident),
            pl.BlockSpec((d_ff, D_MODEL), const, pipeline_mode=resident),
        ],
        out_specs=pl.BlockSpec((None, tm, D_MODEL), tok),
        out_shape=jax.ShapeDtypeStruct((b, s, D_MODEL), F32),
        compiler_params=_cparams(2),
        name=name,
    )(x, oa, yl, gtm, shf, scf, gtf, gpm, gpf, gqf, wo_bf, wi_bf, wf_bf)


def _block_diag(w):
    n, i, j = w.shape
    eye = jnp.eye(n, dtype=w.dtype)
    return (eye[:, None, :, None] * w[:, :, None, :]).reshape(n * i, n * j)


def kernel(x_prompt, x_sample, cache_k, cache_v, state_h, state_conv, page_table, c_prompt, c_sample,
           w_ada, b_ada, g_pre_mix, w_in, lam_q1, lam_k1, lam_q2, lam_k2, g_subln, conv_w, conv_b,
           lru_wa, lru_ba, lru_wx, lru_bx, lru_lambda, w_out, g_post_mix, g_pre_ffn, w_ffn_in,
           w_ffn_out, g_post_ffn):
    depth = w_ada.shape[0]
    bp, seq, _ = x_prompt.shape
    bs = x_sample.shape[0]
    page = cache_k.shape[2]
    past = page_table.shape[1] * page

    pad = (-(bs + bp)) % SUBLANES
    c_all = jnp.concatenate([c_sample, c_prompt, jnp.zeros((pad, D_MODEL), F32)], axis=0)
    mod = _ada_call(c_all, w_ada.astype(BF16), b_ada)

    ys = x_sample.reshape(1, bs, D_MODEL)
    yp = x_prompt

    slope_of = lambda h: 2.0 ** (-8.0 * (h + 1) / N_HEADS)
    slopes = jnp.asarray([slope_of(h) for h in range(N_HEADS)], F32)
    dec_bias = np.zeros((BF16_ROWS, past), np.float32)
    for j in range(2 * N_HEADS):
        dec_bias[j] = -slope_of(j % N_HEADS) * (past - np.arange(past, dtype=np.float32))
    dec_bias = jnp.asarray(dec_bias)

    outs = {n: [] for n in ("kp", "vp", "hp", "cp", "ks", "vs", "hs", "cs")}
    for l in range(depth):
        lam_init = 0.8 - 0.6 * math.exp(-0.3 * l)
        lamv = jnp.stack([lam_q1[l], lam_k1[l], lam_q2[l], lam_k2[l]])
        g_sub = g_subln[l].reshape(1, D_V)
        w_in_bf = w_in[l].astype(BF16)
        wo_bf = w_out[l].astype(BF16)
        wi_bf = w_ffn_in[l].astype(BF16)
        wf_bf = w_ffn_out[l].astype(BF16)
        wg_bf = jnp.concatenate([_block_diag(lru_wa[l]), _block_diag(lru_wx[l])], axis=1).astype(BF16)
        bg = jnp.concatenate([lru_ba[l].reshape(1, D_LRU), lru_bx[l].reshape(1, D_LRU)], axis=1)
        lam_row = lru_lambda[l].reshape(1, D_LRU)
        cb_row = conv_b[l].reshape(1, D_LRU)
        row = lambda v: v[l].reshape(1, D_MODEL)

        mods = [mod[l, :, i * D_MODEL:(i + 1) * D_MODEL] for i in range(6)]
        mod_s = [m[:bs].reshape(1, bs, D_MODEL) for m in mods]
        mod_p = [m[bs:bs + bp].reshape(bp, 1, D_MODEL) for m in mods]

        qt, k4, kb, v4, vt, xr, gr = _inproj_call(yp, mod_p[0], mod_p[1], row(g_pre_mix), w_in_bf,
                                                  TM_PROJ, True, f"inproj_p{l}")
        oa = _attn_call(slopes, qt, kb, vt, lamv, g_sub, lam_init, f"attn_p{l}")
        yl, h8, x8 = _lru_seq_call(xr, gr, conv_w[l], cb_row, wg_bf, bg, lam_row, f"lru_p{l}")
        yp = _outffn_call(yp, oa, yl, mod_p[2], mod_p[3], mod_p[4], mod_p[5],
                          row(g_post_mix), row(g_pre_ffn), row(g_post_ffn),
                          wo_bf, wi_bf, wf_bf, TM_FFN, f"outffn_p{l}")
        outs["kp"].append(k4)
        outs["vp"].append(v4)
        outs["hp"].append(h8[:, SUBLANES - 1])
        outs["cp"].append(x8[:, SUBLANES - (CONV_WIDTH - 1):])

        q4, k4, v4, xr, gr = _inproj_call(ys, mod_s[0], mod_s[1], row(g_pre_mix), w_in_bf,
                                          bs, False, f"inproj_s{l}")
        oa = _decode_attn_call(page_table, q4[0], k4[0], v4[0], dec_bias, lamv, g_sub,
                               cache_k, cache_v, l, lam_init, f"attn_s{l}")
        yl, h_new = _lru_step_call(xr[0], gr[0], jnp.swapaxes(state_conv[l], 0, 1), state_h[l],
                                   conv_w[l], cb_row, wg_bf, bg, lam_row, f"lru_s{l}")
        ys = _outffn_call(ys, oa.reshape(1, bs, D_ATT), yl.reshape(1, bs, D_LRU),
                          mod_s[2], mod_s[3], mod_s[4], mod_s[5],
                          row(g_post_mix), row(g_pre_ffn), row(g_post_ffn),
                          wo_bf, wi_bf, wf_bf, bs, f"outffn_s{l}")
        outs["ks"].append(k4.reshape(bs, 1, N_HEADS, D_V))
        outs["vs"].append(v4.reshape(bs, 1, N_HEADS, D_V))
        outs["hs"].append(h_new)
        outs["cs"].append(jnp.concatenate([state_conv[l][:, 1:], xr[0][:, None, :]], axis=1))

    st = {n: jnp.stack(v) for n, v in outs.items()}
    return (yp, ys.reshape(bs, 1, D_MODEL), st["kp"], st["vp"], st["hp"], st["cp"],
            st["ks"], st["vs"], st["hs"], st["cs"])
```

```python
import functools
import math

import numpy as np
import jax
import jax.numpy as jnp
from jax import lax
from jax.experimental import pallas as pl
from jax.experimental.pallas import tpu as pltpu

D_MODEL = 1024
N_HEADS = 4
D_QK = 64
D_V = 2 * D_QK
D_ATT = N_HEADS * D_V
D_LRU = D_MODEL - D_ATT
N_LRU_BLOCKS = 8
CONV_WIDTH = 4
LRU_C = 8.0
D_IN = 3 * D_ATT + 2 * D_LRU
NORM_EPS = 1e-6
QK_SCALE = D_QK ** -0.5

F32 = jnp.float32
BF16 = jnp.bfloat16

VMEM_LIMIT_BYTES = 56 * 1024 * 1024

SUBLANES = 8
BF16_ROWS = 16
LANES = 128

TM_PROJ = 512
TM_FFN = 256
TQ = TM_PROJ
ATTN_CHUNK = 512
ATTN_AHEAD = 1
T_LRU = 256
ADA_TN = 1536
BF16_EXACT_INT = 256


def _cparams(n_axes):
    return pltpu.CompilerParams(
        dimension_semantics=("arbitrary",) * n_axes,
        vmem_limit_bytes=VMEM_LIMIT_BYTES)


def _rms(x, g):
    return x * lax.rsqrt(jnp.mean(x * x, axis=-1, keepdims=True) + NORM_EPS) * g


def _dot(a, b):
    return jnp.dot(a, b, preferred_element_type=F32)


def _dot_nt(a, b):
    return lax.dot_general(a, b, (((1,), (1,)), ((), ())), preferred_element_type=F32)


def _ada_kernel(c_ref, w_ref, b_ref, o_ref):
    c = c_ref[...]
    h = (c * jax.nn.sigmoid(c)).astype(BF16)
    o_ref[...] = _dot(h, w_ref[...]) + b_ref[...]


def _ada_call(c_all, w_ada_bf, b_ada):
    depth, _, n = w_ada_bf.shape
    rows = c_all.shape[0]
    return pl.pallas_call(
        _ada_kernel,
        grid=(depth, n // ADA_TN),
        in_specs=[
            pl.BlockSpec((rows, D_MODEL), lambda l, j: (0, 0)),
            pl.BlockSpec((None, D_MODEL, ADA_TN), lambda l, j: (l, 0, j)),
            pl.BlockSpec((None, 1, ADA_TN), lambda l, j: (l, 0, j)),
        ],
        out_specs=pl.BlockSpec((None, rows, ADA_TN), lambda l, j: (l, 0, j)),
        out_shape=jax.ShapeDtypeStruct((depth, rows, n), F32),
        compiler_params=_cparams(2),
        name="ada_mod",
    )(c_all, w_ada_bf, b_ada.reshape(depth, 1, n))


def _store_heads(ref, x):
    for h in range(N_HEADS):
        ref[:, h, :] = x[:, h * D_V:(h + 1) * D_V]


def _inproj_kernel(x_ref, sh_ref, sc_ref, g_ref, w_ref, *outs, prompt):
    x = x_ref[...]
    h = (_rms(x, g_ref[...]) * (1.0 + sc_ref[...]) + sh_ref[...]).astype(BF16)
    q = _dot(h, w_ref[:, 0:D_ATT]) * QK_SCALE
    k = _dot(h, w_ref[:, D_ATT:2 * D_ATT])
    v = _dot(h, w_ref[:, 2 * D_ATT:3 * D_ATT])
    if prompt:
        qt_ref, k4_ref, kb_ref, v4_ref, vt_ref, xr_ref, gr_ref = outs
        qt_ref[...] = q.T.astype(BF16)
        kb_ref[...] = k.astype(BF16)
        vt_ref[...] = v.T.astype(BF16)
    else:
        q4_ref, k4_ref, v4_ref, xr_ref, gr_ref = outs
        _store_heads(q4_ref, q)
    _store_heads(k4_ref, k)
    _store_heads(v4_ref, v)
    xr_ref[...] = _dot(h, w_ref[:, 3 * D_ATT:3 * D_ATT + D_LRU])
    gr_ref[...] = _dot(h, w_ref[:, 3 * D_ATT + D_LRU:D_IN])


def _inproj_call(x, sh, sc, g, w_bf, tm, prompt, name):
    b, s, _ = x.shape
    nt = s // tm
    r = sh.shape[1]
    rb = 1 if r == 1 else tm
    tok = lambda i, j: (i, j, 0)
    tok4 = lambda i, j: (i, j, 0, 0)
    mod = (lambda i, j: (i, 0, 0)) if r == 1 else tok
    heads = (jax.ShapeDtypeStruct((b, s, N_HEADS, D_V), F32),
             pl.BlockSpec((None, tm, N_HEADS, D_V), tok4))
    flat32 = (jax.ShapeDtypeStruct((b, s, D_LRU), F32), pl.BlockSpec((None, tm, D_LRU), tok))
    flat16 = (jax.ShapeDtypeStruct((b, s, D_ATT), BF16), pl.BlockSpec((None, tm, D_ATT), tok))
    tiles_t = (jax.ShapeDtypeStruct((b, nt, D_ATT, tm), BF16),
               pl.BlockSpec((None, None, D_ATT, tm), tok4))
    if prompt:
        outs = [tiles_t, heads, flat16, heads, tiles_t, flat32, flat32]
    else:
        outs = [heads, heads, heads, flat32, flat32]
    return pl.pallas_call(
        functools.partial(_inproj_kernel, prompt=prompt),
        grid=(b, nt),
        in_specs=[
            pl.BlockSpec((None, tm, D_MODEL), tok),
            pl.BlockSpec((None, rb, D_MODEL), mod),
            pl.BlockSpec((None, rb, D_MODEL), mod),
            pl.BlockSpec((1, D_MODEL), lambda i, j: (0, 0)),
            pl.BlockSpec((D_MODEL, D_IN), lambda i, j: (0, 0)),
        ],
        out_specs=[o[1] for o in outs],
        out_shape=[o[0] for o in outs],
        compiler_params=_cparams(2),
        name=name,
    )(x, sh, sc, g, w_bf)


def _diff_lambda(lam_ref, lam_init):
    lv = lam_ref[...]
    s1 = jnp.sum(lv[0:1] * lv[1:2], axis=-1, keepdims=True)
    s2 = jnp.sum(lv[2:3] * lv[3:4], axis=-1, keepdims=True)
    return jnp.exp(s1) - jnp.exp(s2) + lam_init


def _alibi_terms(tq):
    off = np.arange(tq)
    lo = (off % BF16_EXACT_INT).astype(np.float32)
    hi = (off - off % BF16_EXACT_INT).astype(np.float32)
    kterm = np.zeros((N_HEADS, tq, LANES), np.float32)
    qterm = np.zeros((N_HEADS, LANES, 2 * tq), np.float32)
    for h in range(N_HEADS):
        slope = 2.0 ** (-8.0 * (h + 1) / N_HEADS)
        kterm[h, :, 0] = slope * lo
        kterm[h, :, 1] = slope * hi
        kterm[h, :, 2:4] = 1.0
        qterm[h, 0:2, :] = 1.0
        qterm[h, 2, :] = np.tile(-slope * lo, 2)
        qterm[h, 3, :] = np.tile(-slope * hi, 2)
    return jnp.asarray(kterm, BF16), jnp.asarray(qterm, BF16)


def _attn_kernel(slope_ref, qt_ref, k_ref, vt_ref, kterm_ref, qterm_ref, lam_ref, g_ref, o_ref,
                 m_sc, acc_sc, s_sc, *, lam_init):
    h = pl.program_id(1)
    qi = pl.program_id(2)
    slope = slope_ref[h]
    tq = TQ

    qt = qt_ref[...]
    zq = jnp.zeros((D_QK, tq), BF16)
    feat = jnp.concatenate(
        [jnp.concatenate([qt[0:D_QK], zq], axis=0), jnp.concatenate([zq, qt[D_QK:D_V]], axis=0)],
        axis=1)
    rhs = jnp.concatenate([feat, qterm_ref[...]], axis=0)
    k_terms = kterm_ref[...]

    ones = jnp.ones((BF16_ROWS, tq), BF16)

    m_sc[...] = jnp.full(m_sc.shape, -jnp.inf, F32)
    acc_sc[...] = jnp.zeros(acc_sc.shape, F32)

    n_chunks = 2 * tq // ATTN_CHUNK
    ahead = ATTN_AHEAD
    chunk = lambda ci: slice(ci * ATTN_CHUNK, (ci + 1) * ATTN_CHUNK)

    def scores(kj, ci):
        start = pl.multiple_of(kj * tq, tq)
        lhs = jnp.concatenate([k_ref[pl.ds(start, tq), :], k_terms], axis=1)
        return _dot(lhs, rhs[:, chunk(ci)])

    def block(kj, masked, last):
        vt = jnp.concatenate([vt_ref[kj], ones], axis=0)
        off = slope * ((qi - kj) * tq).astype(F32)
        s = {ci: s_sc[ci] for ci in range(ahead)}
        for ci in range(n_chunks):
            cs = chunk(ci)
            s_ci = s.pop(ci)
            if masked:
                key = lax.broadcasted_iota(jnp.int32, s_ci.shape, 0)
                qry = (lax.broadcasted_iota(jnp.int32, s_ci.shape, 1) + ci * ATTN_CHUNK) & (tq - 1)
                s_ci = jnp.where(qry >= key, s_ci, -jnp.inf)
            m_prev = m_sc[:, cs]
            m_new = jnp.maximum(m_prev, jnp.max(s_ci, axis=0, keepdims=True) - off)
            p = jnp.exp(s_ci - (m_new + off)).astype(BF16)
            alpha = jnp.exp(m_prev - m_new)
            if ci + ahead < n_chunks:
                s[ci + ahead] = scores(kj, ci + ahead)
            elif not last:
                s_sc[ci + ahead - n_chunks] = scores(kj + 1, ci + ahead - n_chunks)
            acc_sc[:, cs] = alpha * acc_sc[:, cs] + _dot(vt, p)
            m_sc[:, cs] = m_new

    for ci in range(ahead):
        s_sc[ci] = scores(0, ci)

    def pair(i, carry):
        block(2 * i, masked=False, last=False)
        block(2 * i + 1, masked=False, last=False)
        return carry

    lax.fori_loop(0, lax.shift_right_logical(qi, 1), pair, 0)

    @pl.when((qi & 1) == 1)
    def _():
        block(qi - 1, masked=False, last=False)

    block(qi, masked=True, last=True)

    lam = _diff_lambda(lam_ref, lam_init)
    acc = acc_sc[...]
    o1 = acc[0:D_V, 0:tq] / acc[D_V:D_V + 1, 0:tq]
    o2 = acc[0:D_V, tq:2 * tq] / acc[D_V:D_V + 1, tq:2 * tq]
    o = o1 - lam * o2
    o = o * lax.rsqrt(jnp.mean(o * o, axis=0, keepdims=True) + NORM_EPS)
    o_ref[...] = (o.T * g_ref[...] * (1.0 - lam_init)).astype(o_ref.dtype)


def _attn_call(slopes, qt, k_bf, vt, lamv, g_subln, lam_init, name):
    b, nt = qt.shape[0], qt.shape[1]
    s = nt * TQ
    const = lambda i, h, j: (0, 0)
    return pl.pallas_call(
        functools.partial(_attn_kernel, lam_init=lam_init),
        grid=(b, N_HEADS, nt),
        in_specs=[
            pl.BlockSpec(memory_space=pltpu.SMEM),
            pl.BlockSpec((None, None, D_V, TQ), lambda i, h, j: (i, j, h, 0)),
            pl.BlockSpec((None, s, D_V), lambda i, h, j: (i, 0, h)),
            pl.BlockSpec((None, nt, D_V, TQ), lambda i, h, j: (i, 0, h, 0)),
            pl.BlockSpec((None, TQ, LANES), lambda i, h, j: (h, 0, 0)),
            pl.BlockSpec((None, LANES, 2 * TQ), lambda i, h, j: (h, 0, 0)),
            pl.BlockSpec((4, D_QK), const),
            pl.BlockSpec((1, D_V), const),
        ],
        out_specs=pl.BlockSpec((None, TQ, D_V), lambda i, h, j: (i, j, h)),
        out_shape=jax.ShapeDtypeStruct((b, s, D_ATT), BF16),
        scratch_shapes=[
            pltpu.VMEM((1, 2 * TQ), F32),
            pltpu.VMEM((D_V + BF16_ROWS, 2 * TQ), F32),
            pltpu.VMEM((ATTN_AHEAD, TQ, ATTN_CHUNK), F32),
        ],
        compiler_params=_cparams(3),
        name=name,
    )(slopes, qt, k_bf, vt, *_alibi_terms(TQ), lamv, g_subln)


def _dot_tn(a, b):
    return lax.dot_general(a, b, (((0,), (0,)), ((), ())), preferred_element_type=F32)


def _decode_attn_kernel(pt_ref, q_ref, kn_ref, vn_ref, bias_ref, lam_ref, g_ref, *rest,
                        n_pages, lam_init):
    del pt_ref
    k_pages = rest[:n_pages]
    v_pages = rest[n_pages:2 * n_pages]
    o_ref = rest[2 * n_pages]
    page = k_pages[0].shape[0] // N_HEADS
    cols = BF16_ROWS

    def flat(ref):
        return jnp.concatenate([ref[h:h + 1, :] for h in range(N_HEADS)], axis=1)

    def dense(ref):
        return jnp.concatenate(
            [ref[pl.ds(h, page, stride=N_HEADS), :] for h in range(N_HEADS)], axis=1).astype(BF16)

    def first_row(x):
        r = lax.broadcasted_iota(jnp.int32, (BF16_ROWS, D_ATT), 0)
        return jnp.where(r == 0, jnp.broadcast_to(x, (BF16_ROWS, D_ATT)), 0.0).astype(BF16)

    r = lax.broadcasted_iota(jnp.int32, (LANES, D_ATT), 0)
    c = lax.broadcasted_iota(jnp.int32, (LANES, D_ATT), 1)
    col = ((c >> 6) & 1) * N_HEADS + (c >> 7)
    qt = jnp.where(r == col, jnp.broadcast_to(flat(q_ref), (LANES, D_ATT)), 0.0).astype(BF16)

    s = jnp.concatenate([_dot_nt(dense(kp), qt) for kp in k_pages], axis=0)
    s = s + bias_ref[...]
    s_new = _dot_nt(first_row(flat(kn_ref)), qt)
    s_new = jnp.where(lax.broadcasted_iota(jnp.int32, s_new.shape, 0) == 0, s_new, -jnp.inf)

    m = jnp.maximum(jnp.max(s, axis=0, keepdims=True), jnp.max(s_new, axis=0, keepdims=True))
    e = jnp.exp(s - m)
    e_new = jnp.exp(s_new - m)
    l = jnp.sum(e, axis=0, keepdims=True) + jnp.sum(e_new, axis=0, keepdims=True)
    p = (e / l)[:, 0:cols].astype(BF16)
    p_new = (e_new / l)[:, 0:cols].astype(BF16)

    acc = _dot_tn(p_new, first_row(flat(vn_ref)))
    for i, vp in enumerate(v_pages):
        acc = acc + _dot_tn(p[i * page:(i + 1) * page], dense(vp))

    lam = _diff_lambda(lam_ref, lam_init)
    for h in range(N_HEADS):
        o = (acc[h:h + 1, h * D_V:(h + 1) * D_V]
             - lam * acc[N_HEADS + h:N_HEADS + h + 1, h * D_V:(h + 1) * D_V])
        o_ref[:, h * D_V:(h + 1) * D_V] = (_rms(o, g_ref[...]) * (1.0 - lam_init)).astype(o_ref.dtype)


def _decode_attn_call(page_table, q4, k4, v4, bias, lamv, g_subln, cache_k, cache_v,
                      layer, lam_init, name):
    b = q4.shape[0]
    n_pages = page_table.shape[1]
    slab = cache_k.shape[2]
    tok = lambda i, pt: (i, 0, 0)
    const = lambda i, pt: (0, 0)

    def page_spec(j):
        return pl.BlockSpec((None, None, slab, D_V), lambda i, pt, j=j: (layer, pt[i, j], 0, 0))

    grid_spec = pltpu.PrefetchScalarGridSpec(
        num_scalar_prefetch=1,
        grid=(b,),
        in_specs=[
            pl.BlockSpec((None, N_HEADS, D_V), tok),
            pl.BlockSpec((None, N_HEADS, D_V), tok),
            pl.BlockSpec((None, N_HEADS, D_V), tok),
            pl.BlockSpec(bias.shape, const),
            pl.BlockSpec((4, D_QK), const),
            pl.BlockSpec((1, D_V), const),
        ] + [page_spec(j) for j in range(n_pages)] + [page_spec(j) for j in range(n_pages)],
        out_specs=pl.BlockSpec((None, 1, D_ATT), tok),
    )
    return pl.pallas_call(
        functools.partial(_decode_attn_kernel, n_pages=n_pages, lam_init=lam_init),
        grid_spec=grid_spec,
        out_shape=jax.ShapeDtypeStruct((b, 1, D_ATT), BF16),
        compiler_params=_cparams(1),
        name=name,
    )(page_table, q4, k4, v4, bias, lamv, g_subln,
      *([cache_k] * n_pages), *([cache_v] * n_pages))


def _lru_gates(u, wg_ref, bg_ref, lam_ref):
    gates = jax.nn.sigmoid(_dot(u.astype(BF16), wg_ref[...]) + bg_ref[...])
    r = gates[:, :D_LRU]
    i_g = gates[:, D_LRU:]
    neg_lam = -lam_ref[...]
    softplus = jnp.maximum(neg_lam, 0.0) + jnp.log1p(jnp.exp(-jnp.abs(neg_lam)))
    log_a = -LRU_C * r * softplus
    a = jnp.exp(log_a)
    one_minus_a2 = -jnp.tanh(log_a) * (a * a + 1.0)
    return a, jnp.sqrt(one_minus_a2) * (i_g * u)


def _lru_seq_kernel(xr_ref, gr_ref, cw_ref, cb_ref, wg_ref, bg_ref, lam_ref,
                    y_ref, h_ref, tail_ref, xbuf, hbuf):
    t = T_LRU
    j = pl.program_id(1)

    @pl.when(j == 0)
    def _():
        xbuf[0:SUBLANES, :] = jnp.zeros((SUBLANES, D_LRU), F32)
        hbuf[...] = jnp.zeros(hbuf.shape, F32)

    x = xr_ref[...]
    xbuf[SUBLANES:SUBLANES + t, :] = x
    cw = cw_ref[...]
    u = cb_ref[...] + cw[CONV_WIDTH - 1:CONV_WIDTH] * x
    for d in range(1, CONV_WIDTH):
        u = u + cw[CONV_WIDTH - 1 - d:CONV_WIDTH - d] * xbuf[SUBLANES - d:SUBLANES - d + t, :]

    a, b = _lru_gates(u, wg_ref, bg_ref, lam_ref)

    row = lax.broadcasted_iota(jnp.int32, (t, D_LRU), 0)
    shift = 1
    while shift < t:
        a_prev = pltpu.roll(a, shift, 0)
        b_prev = pltpu.roll(b, shift, 0)
        live = row >= shift
        b = jnp.where(live, a * b_prev + b, b)
        a = jnp.where(live, a * a_prev, a)
        shift *= 2

    h0 = hbuf[SUBLANES - 1:SUBLANES, :]
    hs = b + a * h0
    y_ref[...] = (hs * jax.nn.gelu(gr_ref[...])).astype(y_ref.dtype)

    hbuf[...] = hs[t - SUBLANES:t]
    xbuf[0:SUBLANES, :] = x[t - SUBLANES:t]
    h_ref[...] = hs[t - SUBLANES:t]
    tail_ref[...] = x[t - SUBLANES:t]


def _lru_seq_call(xr, gr, conv_w, conv_b, wg_bf, bg, lam, name):
    b, s, _ = xr.shape
    tok = lambda i, j: (i, j, 0)
    const = lambda i, j: (0, 0)
    per_seq = lambda i, j: (i, 0, 0)
    return pl.pallas_call(
        _lru_seq_kernel,
        grid=(b, s // T_LRU),
        in_specs=[
            pl.BlockSpec((None, T_LRU, D_LRU), tok),
            pl.BlockSpec((None, T_LRU, D_LRU), tok),
            pl.BlockSpec((CONV_WIDTH, D_LRU), const),
            pl.BlockSpec((1, D_LRU), const),
            pl.BlockSpec((D_LRU, 2 * D_LRU), const),
            pl.BlockSpec((1, 2 * D_LRU), const),
            pl.BlockSpec((1, D_LRU), const),
        ],
        out_specs=[
            pl.BlockSpec((None, T_LRU, D_LRU), tok),
            pl.BlockSpec((None, SUBLANES, D_LRU), per_seq),
            pl.BlockSpec((None, SUBLANES, D_LRU), per_seq),
        ],
        out_shape=[
            jax.ShapeDtypeStruct((b, s, D_LRU), BF16),
            jax.ShapeDtypeStruct((b, SUBLANES, D_LRU), F32),
            jax.ShapeDtypeStruct((b, SUBLANES, D_LRU), F32),
        ],
        scratch_shapes=[
            pltpu.VMEM((SUBLANES + T_LRU, D_LRU), F32),
            pltpu.VMEM((SUBLANES, D_LRU), F32),
        ],
        compiler_params=_cparams(2),
        name=name,
    )(xr, gr, conv_w, conv_b, wg_bf, bg, lam)


def _lru_step_kernel(xr_ref, gr_ref, cbuf_ref, h0_ref, cw_ref, cb_ref, wg_ref, bg_ref, lam_ref,
                     y_ref, h_ref):
    x = xr_ref[...]
    cw = cw_ref[...]
    u = cb_ref[...] + cw[CONV_WIDTH - 1:CONV_WIDTH] * x
    for d in range(CONV_WIDTH - 1):
        u = u + cw[d:d + 1] * cbuf_ref[d]
    a, b = _lru_gates(u, wg_ref, bg_ref, lam_ref)
    h = a * h0_ref[...] + b
    h_ref[...] = h
    y_ref[...] = (h * jax.nn.gelu(gr_ref[...])).astype(y_ref.dtype)


def _lru_step_call(xr, gr, cbuf_t, h0, conv_w, conv_b, wg_bf, bg, lam, name):
    b = xr.shape[0]
    return pl.pallas_call(
        _lru_step_kernel,
        out_shape=[jax.ShapeDtypeStruct((b, D_LRU), BF16),
                   jax.ShapeDtypeStruct((b, D_LRU), F32)],
        compiler_params=pltpu.CompilerParams(vmem_limit_bytes=VMEM_LIMIT_BYTES),
        name=name,
    )(xr, gr, cbuf_t, h0, conv_w, conv_b, wg_bf, bg, lam)


def _outffn_kernel(x_ref, oa_ref, yl_ref, gtm_ref, shf_ref, scf_ref, gtf_ref,
                   gpm_ref, gpf_ref, gqf_ref, wo_ref, wi_ref, wf_ref, o_ref):
    d_ff = wf_ref.shape[0]
    mix = _dot(oa_ref[...], wo_ref[0:D_ATT, :]) + _dot(yl_ref[...], wo_ref[D_ATT:D_MODEL, :])
    x1 = x_ref[...] + gtm_ref[...] * _rms(mix, gpm_ref[...])
    h = (_rms(x1, gpf_ref[...]) * (1.0 + scf_ref[...]) + shf_ref[...]).astype(BF16)
    gate = _dot(h, wi_ref[:, 0:d_ff])
    up = _dot(h, wi_ref[:, d_ff:2 * d_ff])
    act = (gate * jax.nn.sigmoid(gate) * up).astype(BF16)
    f = _dot(act, wf_ref[...])
    o_ref[...] = x1 + gtf_ref[...] * _rms(f, gqf_ref[...])


def _outffn_call(x, oa, yl, gtm, shf, scf, gtf, gpm, gpf, gqf, wo_bf, wi_bf, wf_bf, tm, name):
    b, s, _ = x.shape
    r = gtm.shape[1]
    rb = 1 if r == 1 else tm
    d_ff = wf_bf.shape[0]
    tok = lambda i, j: (i, j, 0)
    mod = (lambda i, j: (i, 0, 0)) if r == 1 else tok
    const = lambda i, j: (0, 0)
    resident = pl.Buffered(1)
    return pl.pallas_call(
        _outffn_kernel,
        grid=(b, s // tm),
        in_specs=[
            pl.BlockSpec((None, tm, D_MODEL), tok),
            pl.BlockSpec((None, tm, D_ATT), tok),
            pl.BlockSpec((None, tm, D_LRU), tok),
            pl.BlockSpec((None, rb, D_MODEL), mod),
            pl.BlockSpec((None, rb, D_MODEL), mod),
            pl.BlockSpec((None, rb, D_MODEL), mod),
            pl.BlockSpec((None, rb, D_MODEL), mod),
            pl.BlockSpec((1, D_MODEL), const),
            pl.BlockSpec((1, D_MODEL), const),
            pl.BlockSpec((1, D_MODEL), const),
            pl.BlockSpec((D_MODEL, D_MODEL), const, pipeline_mode=resident),
            pl.BlockSpec((D_MODEL, 2 * d_ff), const, pipeline_mode=resident),
            pl.BlockSpec((d_ff, D_MODEL), const, pipeline_mode=resident),
        ],
        out_specs=pl.BlockSpec((None, tm, D_MODEL), tok),
        out_shape=jax.ShapeDtypeStruct((b, s, D_MODEL), F32),
        compiler_params=_cparams(2),
        name=name,
    )(x, oa, yl, gtm, shf, scf, gtf, gpm, gpf, gqf, wo_bf, wi_bf, wf_bf)


def _block_diag(w):
    n, i, j = w.shape
    eye = jnp.eye(n, dtype=w.dtype)
    return (eye[:, None, :, None] * w[:, :, None, :]).reshape(n * i, n * j)


def kernel(x_prompt, x_sample, cache_k, cache_v, state_h, state_conv, page_table, c_prompt, c_sample,
           w_ada, b_ada, g_pre_mix, w_in, lam_q1, lam_k1, lam_q2, lam_k2, g_subln, conv_w, conv_b,
           lru_wa, lru_ba, lru_wx, lru_bx, lru_lambda, w_out, g_post_mix, g_pre_ffn, w_ffn_in,
           w_ffn_out, g_post_ffn):
    depth = w_ada.shape[0]
    bp, seq, _ = x_prompt.shape
    bs = x_sample.shape[0]
    page = cache_k.shape[2]
    past = page_table.shape[1] * page

    pad = (-(bs + bp)) % SUBLANES
    c_all = jnp.concatenate([c_sample, c_prompt, jnp.zeros((pad, D_MODEL), F32)], axis=0)
    mod = _ada_call(c_all, w_ada.astype(BF16), b_ada)

    ys = x_sample.reshape(1, bs, D_MODEL)
    yp = x_prompt

    slope_of = lambda h: 2.0 ** (-8.0 * (h + 1) / N_HEADS)
    slopes = jnp.asarray([slope_of(h) for h in range(N_HEADS)], F32)
    dec_bias = np.zeros((past, LANES), np.float32)
    for j in range(2 * N_HEADS):
        dec_bias[:, j] = -slope_of(j % N_HEADS) * (past - np.arange(past, dtype=np.float32))
    dec_bias = jnp.asarray(dec_bias)
    ck = cache_k.reshape(depth, cache_k.shape[1], page * N_HEADS, D_V)
    cv = cache_v.reshape(depth, cache_v.shape[1], page * N_HEADS, D_V)

    outs = {n: [] for n in ("kp", "vp", "hp", "cp", "ks", "vs", "hs", "cs")}
    for l in range(depth):
        lam_init = 0.8 - 0.6 * math.exp(-0.3 * l)
        lamv = jnp.stack([lam_q1[l], lam_k1[l], lam_q2[l], lam_k2[l]])
        g_sub = g_subln[l].reshape(1, D_V)
        w_in_bf = w_in[l].astype(BF16)
        wo_bf = w_out[l].astype(BF16)
        wi_bf = w_ffn_in[l].astype(BF16)
        wf_bf = w_ffn_out[l].astype(BF16)
        wg_bf = jnp.concatenate([_block_diag(lru_wa[l]), _block_diag(lru_wx[l])], axis=1).astype(BF16)
        bg = jnp.concatenate([lru_ba[l].reshape(1, D_LRU), lru_bx[l].reshape(1, D_LRU)], axis=1)
        lam_row = lru_lambda[l].reshape(1, D_LRU)
        cb_row = conv_b[l].reshape(1, D_LRU)
        row = lambda v: v[l].reshape(1, D_MODEL)

        mods = [mod[l, :, i * D_MODEL:(i + 1) * D_MODEL] for i in range(6)]
        mod_s = [m[:bs].reshape(1, bs, D_MODEL) for m in mods]
        mod_p = [m[bs:bs + bp].reshape(bp, 1, D_MODEL) for m in mods]

        qt, k4, kb, v4, vt, xr, gr = _inproj_call(yp, mod_p[0], mod_p[1], row(g_pre_mix), w_in_bf,
                                                  TM_PROJ, True, f"inproj_p{l}")
        oa = _attn_call(slopes, qt, kb, vt, lamv, g_sub, lam_init, f"attn_p{l}")
        yl, h8, x8 = _lru_seq_call(xr, gr, conv_w[l], cb_row, wg_bf, bg, lam_row, f"lru_p{l}")
        yp = _outffn_call(yp, oa, yl, mod_p[2], mod_p[3], mod_p[4], mod_p[5],
                          row(g_post_mix), row(g_pre_ffn), row(g_post_ffn),
                          wo_bf, wi_bf, wf_bf, TM_FFN, f"outffn_p{l}")
        outs["kp"].append(k4)
        outs["vp"].append(v4)
        outs["hp"].append(h8[:, SUBLANES - 1])
        outs["cp"].append(x8[:, SUBLANES - (CONV_WIDTH - 1):])

        q4, k4, v4, xr, gr = _inproj_call(ys, mod_s[0], mod_s[1], row(g_pre_mix), w_in_bf,
                                          bs, False, f"inproj_s{l}")
        oa = _decode_attn_call(page_table, q4[0], k4[0], v4[0], dec_bias, lamv, g_sub,
                               ck, cv, l, lam_init, f"attn_s{l}")
        yl, h_new = _lru_step_call(xr[0], gr[0], jnp.swapaxes(state_conv[l], 0, 1), state_h[l],
                                   conv_w[l], cb_row, wg_bf, bg, lam_row, f"lru_s{l}")
        ys = _outffn_call(ys, oa.reshape(1, bs, D_ATT), yl.reshape(1, bs, D_LRU),
                          mod_s[2], mod_s[3], mod_s[4], mod_s[5],
                          row(g_post_mix), row(g_pre_ffn), row(g_post_ffn),
                          wo_bf, wi_bf, wf_bf, bs, f"outffn_s{l}")
        outs["ks"].append(k4.reshape(bs, 1, N_HEADS, D_V))
        outs["vs"].append(v4.reshape(bs, 1, N_HEADS, D_V))
        outs["hs"].append(h_new)
        outs["cs"].append(jnp.concatenate([state_conv[l][:, 1:], xr[0][:, None, :]], axis=1))

    st = {n: jnp.stack(v) for n, v in outs.items()}
    return (yp, ys.reshape(bs, 1, D_MODEL), st["kp"], st["vp"], st["hp"], st["cp"],
            st["ks"], st["vs"], st["hs"], st["cs"])
```

```python
import functools
import math

import numpy as np
import jax
import jax.numpy as jnp
from jax import lax
from jax.experimental import pallas as pl
from jax.experimental.pallas import tpu as pltpu

D_MODEL = 1024
N_HEADS = 4
D_QK = 64
D_V = 2 * D_QK
D_ATT = N_HEADS * D_V
D_LRU = D_MODEL - D_ATT
N_LRU_BLOCKS = 8
CONV_WIDTH = 4
LRU_C = 8.0
D_IN = 3 * D_ATT + 2 * D_LRU
NORM_EPS = 1e-6
QK_SCALE = D_QK ** -0.5

F32 = jnp.float32
BF16 = jnp.bfloat16

VMEM_LIMIT_BYTES = 56 * 1024 * 1024

SUBLANES = 8
BF16_ROWS = 16
LANES = 128

TM_PROJ = 512
TM_FFN = 512
TQ = TM_PROJ
ATTN_CHUNK = 512
T_LRU = 256
ADA_TN = 1536
BF16_EXACT_INT = 256


def _cparams(n_axes):
    return pltpu.CompilerParams(
        dimension_semantics=("arbitrary",) * n_axes,
        vmem_limit_bytes=VMEM_LIMIT_BYTES)


def _rms(x, g):
    return x * lax.rsqrt(jnp.mean(x * x, axis=-1, keepdims=True) + NORM_EPS) * g


def _dot(a, b):
    return jnp.dot(a, b, preferred_element_type=F32)


def _dot_nt(a, b):
    return lax.dot_general(a, b, (((1,), (1,)), ((), ())), preferred_element_type=F32)


def _ada_kernel(c_ref, w_ref, b_ref, o_ref):
    c = c_ref[...]
    h = (c * jax.nn.sigmoid(c)).astype(BF16)
    o_ref[...] = _dot(h, w_ref[...]) + b_ref[...]


def _ada_call(c_all, w_ada_bf, b_ada):
    depth, _, n = w_ada_bf.shape
    rows = c_all.shape[0]
    return pl.pallas_call(
        _ada_kernel,
        grid=(depth, n // ADA_TN),
        in_specs=[
            pl.BlockSpec((rows, D_MODEL), lambda l, j: (0, 0)),
            pl.BlockSpec((None, D_MODEL, ADA_TN), lambda l, j: (l, 0, j)),
            pl.BlockSpec((None, 1, ADA_TN), lambda l, j: (l, 0, j)),
        ],
        out_specs=pl.BlockSpec((None, rows, ADA_TN), lambda l, j: (l, 0, j)),
        out_shape=jax.ShapeDtypeStruct((depth, rows, n), F32),
        compiler_params=_cparams(2),
        name="ada_mod",
    )(c_all, w_ada_bf, b_ada.reshape(depth, 1, n))


def _store_heads(ref, x):
    tm = x.shape[0]
    for h in range(N_HEADS):
        ref[pl.ds(h, tm, stride=N_HEADS), :] = x[:, h * D_V:(h + 1) * D_V]


def _inproj_kernel(x_ref, sh_ref, sc_ref, g_ref, w_ref, *outs, prompt):
    x = x_ref[...]
    h = (_rms(x, g_ref[...]) * (1.0 + sc_ref[...]) + sh_ref[...]).astype(BF16)
    q = _dot(h, w_ref[:, 0:D_ATT]) * QK_SCALE
    k = _dot(h, w_ref[:, D_ATT:2 * D_ATT])
    v = _dot(h, w_ref[:, 2 * D_ATT:3 * D_ATT])
    if prompt:
        qt_ref, k4_ref, kb_ref, v4_ref, vt_ref, xr_ref, gr_ref = outs
        qt_ref[...] = q.T.astype(BF16)
        kb_ref[...] = k.astype(BF16)
        vt_ref[...] = v.T.astype(BF16)
    else:
        q4_ref, k4_ref, v4_ref, xr_ref, gr_ref = outs
        _store_heads(q4_ref, q)
    _store_heads(k4_ref, k)
    _store_heads(v4_ref, v)
    xr_ref[...] = _dot(h, w_ref[:, 3 * D_ATT:3 * D_ATT + D_LRU])
    gr_ref[...] = _dot(h, w_ref[:, 3 * D_ATT + D_LRU:D_IN])


def _inproj_call(x, sh, sc, g, w_bf, tm, prompt, name):
    b, s, _ = x.shape
    nt = s // tm
    r = sh.shape[1]
    rb = 1 if r == 1 else tm
    tok = lambda i, j: (i, j, 0)
    tok4 = lambda i, j: (i, j, 0, 0)
    mod = (lambda i, j: (i, 0, 0)) if r == 1 else tok
    heads = (jax.ShapeDtypeStruct((b, s * N_HEADS, D_V), F32),
             pl.BlockSpec((None, tm * N_HEADS, D_V), tok))
    flat32 = (jax.ShapeDtypeStruct((b, s, D_LRU), F32), pl.BlockSpec((None, tm, D_LRU), tok))
    flat16 = (jax.ShapeDtypeStruct((b, s, D_ATT), BF16), pl.BlockSpec((None, tm, D_ATT), tok))
    tiles_t = (jax.ShapeDtypeStruct((b, nt, D_ATT, tm), BF16),
               pl.BlockSpec((None, None, D_ATT, tm), tok4))
    if prompt:
        outs = [tiles_t, heads, flat16, heads, tiles_t, flat32, flat32]
    else:
        outs = [heads, heads, heads, flat32, flat32]
    return pl.pallas_call(
        functools.partial(_inproj_kernel, prompt=prompt),
        grid=(b, nt),
        in_specs=[
            pl.BlockSpec((None, tm, D_MODEL), tok),
            pl.BlockSpec((None, rb, D_MODEL), mod),
            pl.BlockSpec((None, rb, D_MODEL), mod),
            pl.BlockSpec((1, D_MODEL), lambda i, j: (0, 0)),
            pl.BlockSpec((D_MODEL, D_IN), lambda i, j: (0, 0)),
        ],
        out_specs=[o[1] for o in outs],
        out_shape=[o[0] for o in outs],
        compiler_params=_cparams(2),
        name=name,
    )(x, sh, sc, g, w_bf)


def _diff_lambda(lam_ref, lam_init):
    lv = lam_ref[...]
    s1 = jnp.sum(lv[0:1] * lv[1:2], axis=-1, keepdims=True)
    s2 = jnp.sum(lv[2:3] * lv[3:4], axis=-1, keepdims=True)
    return jnp.exp(s1) - jnp.exp(s2) + lam_init


def _alibi_terms(tq):
    off = np.arange(tq)
    lo = (off % BF16_EXACT_INT).astype(np.float32)
    hi = (off - off % BF16_EXACT_INT).astype(np.float32)
    kterm = np.zeros((N_HEADS, tq, LANES), np.float32)
    qterm = np.zeros((N_HEADS, LANES, 2 * tq), np.float32)
    for h in range(N_HEADS):
        slope = 2.0 ** (-8.0 * (h + 1) / N_HEADS)
        kterm[h, :, 0] = slope * lo
        kterm[h, :, 1] = slope * hi
        kterm[h, :, 2:4] = 1.0
        qterm[h, 0:2, :] = 1.0
        qterm[h, 2, :] = np.tile(-slope * lo, 2)
        qterm[h, 3, :] = np.tile(-slope * hi, 2)
    return jnp.asarray(kterm, BF16), jnp.asarray(qterm, BF16)


def _attn_kernel(slope_ref, qt_ref, k_ref, vt_ref, kterm_ref, qterm_ref, lam_ref, g_ref, o_ref,
                 m_sc, acc_sc, s_sc, *, lam_init):
    h = pl.program_id(1)
    qi = pl.program_id(2)
    slope = slope_ref[h]
    tq = TQ

    qt = qt_ref[...]
    zq = jnp.zeros((D_QK, tq), BF16)
    feat = jnp.concatenate(
        [jnp.concatenate([qt[0:D_QK], zq], axis=0), jnp.concatenate([zq, qt[D_QK:D_V]], axis=0)],
        axis=1)
    rhs = jnp.concatenate([feat, qterm_ref[...]], axis=0)
    k_terms = kterm_ref[...]

    ones = jnp.ones((BF16_ROWS, tq), BF16)

    m_sc[...] = jnp.full(m_sc.shape, -jnp.inf, F32)
    acc_sc[...] = jnp.zeros(acc_sc.shape, F32)

    n_chunks = 2 * tq // ATTN_CHUNK
    chunk = lambda ci: slice(ci * ATTN_CHUNK, (ci + 1) * ATTN_CHUNK)

    def scores(kj, ci):
        start = pl.multiple_of(kj * tq, tq)
        lhs = jnp.concatenate([k_ref[pl.ds(start, tq), :], k_terms], axis=1)
        return _dot(lhs, rhs[:, chunk(ci)])

    def block(kj, rd, masked, last):
        vt = jnp.concatenate([vt_ref[kj], ones], axis=0)
        off = slope * ((qi - kj) * tq).astype(F32)
        for ci in range(n_chunks):
            cs = chunk(ci)
            s_ci = s_sc[rd, ci]
            if masked:
                key = lax.broadcasted_iota(jnp.int32, s_ci.shape, 0)
                qry = (lax.broadcasted_iota(jnp.int32, s_ci.shape, 1) + ci * ATTN_CHUNK) & (tq - 1)
                s_ci = jnp.where(qry >= key, s_ci, -jnp.inf)
            m_prev = m_sc[:, cs]
            m_new = jnp.maximum(m_prev, jnp.max(s_ci, axis=0, keepdims=True) - off)
            p = jnp.exp(s_ci - (m_new + off)).astype(BF16)
            alpha = jnp.exp(m_prev - m_new)
            if not last:
                s_sc[1 - rd, ci] = scores(kj + 1, ci)
            acc_sc[:, cs] = alpha * acc_sc[:, cs] + _dot(vt, p)
            m_sc[:, cs] = m_new

    for ci in range(n_chunks):
        s_sc[0, ci] = scores(0, ci)

    def pair(i, carry):
        block(2 * i, 0, masked=False, last=False)
        block(2 * i + 1, 1, masked=False, last=False)
        return carry

    lax.fori_loop(0, lax.shift_right_logical(qi, 1), pair, 0)

    @pl.when((qi & 1) == 1)
    def _():
        block(qi - 1, 0, masked=False, last=False)
        block(qi, 1, masked=True, last=True)

    @pl.when((qi & 1) == 0)
    def _():
        block(qi, 0, masked=True, last=True)

    lam = _diff_lambda(lam_ref, lam_init)
    acc = acc_sc[...]
    o1 = acc[0:D_V, 0:tq] / acc[D_V:D_V + 1, 0:tq]
    o2 = acc[0:D_V, tq:2 * tq] / acc[D_V:D_V + 1, tq:2 * tq]
    o = o1 - lam * o2
    o = o * lax.rsqrt(jnp.mean(o * o, axis=0, keepdims=True) + NORM_EPS)
    o_ref[...] = (o.T * g_ref[...] * (1.0 - lam_init)).astype(o_ref.dtype)


def _attn_call(slopes, qt, k_bf, vt, lamv, g_subln, lam_init, name):
    b, nt = qt.shape[0], qt.shape[1]
    s = nt * TQ
    const = lambda i, h, j: (0, 0)
    return pl.pallas_call(
        functools.partial(_attn_kernel, lam_init=lam_init),
        grid=(b, N_HEADS, nt),
        in_specs=[
            pl.BlockSpec(memory_space=pltpu.SMEM),
            pl.BlockSpec((None, None, D_V, TQ), lambda i, h, j: (i, j, h, 0)),
            pl.BlockSpec((None, s, D_V), lambda i, h, j: (i, 0, h)),
            pl.BlockSpec((None, nt, D_V, TQ), lambda i, h, j: (i, 0, h, 0)),
            pl.BlockSpec((None, TQ, LANES), lambda i, h, j: (h, 0, 0)),
            pl.BlockSpec((None, LANES, 2 * TQ), lambda i, h, j: (h, 0, 0)),
            pl.BlockSpec((4, D_QK), const),
            pl.BlockSpec((1, D_V), const),
        ],
        out_specs=pl.BlockSpec((None, TQ, D_V), lambda i, h, j: (i, j, h)),
        out_shape=jax.ShapeDtypeStruct((b, s, D_ATT), BF16),
        scratch_shapes=[
            pltpu.VMEM((1, 2 * TQ), F32),
            pltpu.VMEM((D_V + BF16_ROWS, 2 * TQ), F32),
            pltpu.VMEM((2, 2 * TQ // ATTN_CHUNK, TQ, ATTN_CHUNK), F32),
        ],
        compiler_params=_cparams(3),
        name=name,
    )(slopes, qt, k_bf, vt, *_alibi_terms(TQ), lamv, g_subln)


def _dot_tn(a, b):
    return lax.dot_general(a, b, (((0,), (0,)), ((), ())), preferred_element_type=F32)


def _decode_attn_kernel(pt_ref, q_ref, kn_ref, vn_ref, bias_ref, lam_ref, g_ref, *rest,
                        n_pages, lam_init):
    del pt_ref
    k_pages = rest[:n_pages]
    v_pages = rest[n_pages:2 * n_pages]
    o_ref = rest[2 * n_pages]
    page = k_pages[0].shape[0] // N_HEADS
    cols = BF16_ROWS

    def flat(ref):
        return jnp.concatenate([ref[h:h + 1, :] for h in range(N_HEADS)], axis=1)

    def dense(ref):
        return jnp.concatenate(
            [ref[pl.ds(h, page, stride=N_HEADS), :] for h in range(N_HEADS)], axis=1).astype(BF16)

    def first_row(x):
        r = lax.broadcasted_iota(jnp.int32, (BF16_ROWS, D_ATT), 0)
        return jnp.where(r == 0, jnp.broadcast_to(x, (BF16_ROWS, D_ATT)), 0.0).astype(BF16)

    r = lax.broadcasted_iota(jnp.int32, (LANES, D_ATT), 0)
    c = lax.broadcasted_iota(jnp.int32, (LANES, D_ATT), 1)
    col = ((c >> 6) & 1) * N_HEADS + (c >> 7)
    qt = jnp.where(r == col, jnp.broadcast_to(flat(q_ref), (LANES, D_ATT)), 0.0).astype(BF16)

    s = jnp.concatenate([_dot_nt(dense(kp), qt) for kp in k_pages], axis=0)
    s = s + bias_ref[...]
    s_new = _dot_nt(first_row(flat(kn_ref)), qt)
    s_new = jnp.where(lax.broadcasted_iota(jnp.int32, s_new.shape, 0) == 0, s_new, -jnp.inf)

    m = jnp.maximum(jnp.max(s, axis=0, keepdims=True), jnp.max(s_new, axis=0, keepdims=True))
    e = jnp.exp(s - m)
    e_new = jnp.exp(s_new - m)
    l = jnp.sum(e, axis=0, keepdims=True) + jnp.sum(e_new, axis=0, keepdims=True)
    p = (e / l)[:, 0:cols].astype(BF16)
    p_new = (e_new / l)[:, 0:cols].astype(BF16)

    acc = _dot_tn(p_new, first_row(flat(vn_ref)))
    for i, vp in enumerate(v_pages):
        acc = acc + _dot_tn(p[i * page:(i + 1) * page], dense(vp))

    lam = _diff_lambda(lam_ref, lam_init)
    for h in range(N_HEADS):
        o = (acc[h:h + 1, h * D_V:(h + 1) * D_V]
             - lam * acc[N_HEADS + h:N_HEADS + h + 1, h * D_V:(h + 1) * D_V])
        o_ref[:, h * D_V:(h + 1) * D_V] = (_rms(o, g_ref[...]) * (1.0 - lam_init)).astype(o_ref.dtype)


def _decode_attn_call(page_table, q4, k4, v4, bias, lamv, g_subln, cache_k, cache_v,
                      layer, lam_init, name):
    b = q4.shape[0]
    n_pages = page_table.shape[1]
    slab = cache_k.shape[2]
    tok = lambda i, pt: (i, 0, 0)
    const = lambda i, pt: (0, 0)

    def page_spec(j):
        return pl.BlockSpec((None, None, slab, D_V), lambda i, pt, j=j: (layer, pt[i, j], 0, 0))

    grid_spec = pltpu.PrefetchScalarGridSpec(
        num_scalar_prefetch=1,
        grid=(b,),
        in_specs=[
            pl.BlockSpec((None, N_HEADS, D_V), tok),
            pl.BlockSpec((None, N_HEADS, D_V), tok),
            pl.BlockSpec((None, N_HEADS, D_V), tok),
            pl.BlockSpec(bias.shape, const),
            pl.BlockSpec((4, D_QK), const),
            pl.BlockSpec((1, D_V), const),
        ] + [page_spec(j) for j in range(n_pages)] + [page_spec(j) for j in range(n_pages)],
        out_specs=pl.BlockSpec((None, 1, D_ATT), tok),
    )
    return pl.pallas_call(
        functools.partial(_decode_attn_kernel, n_pages=n_pages, lam_init=lam_init),
        grid_spec=grid_spec,
        out_shape=jax.ShapeDtypeStruct((b, 1, D_ATT), BF16),
        compiler_params=_cparams(1),
        name=name,
    )(page_table, q4, k4, v4, bias, lamv, g_subln,
      *([cache_k] * n_pages), *([cache_v] * n_pages))


def _lru_gates(u, wg_ref, bg_ref, lam_ref):
    gates = jax.nn.sigmoid(_dot(u.astype(BF16), wg_ref[...]) + bg_ref[...])
    r = gates[:, :D_LRU]
    i_g = gates[:, D_LRU:]
    neg_lam = -lam_ref[...]
    softplus = jnp.maximum(neg_lam, 0.0) + jnp.log1p(jnp.exp(-jnp.abs(neg_lam)))
    log_a = -LRU_C * r * softplus
    a = jnp.exp(log_a)
    one_minus_a2 = -jnp.tanh(log_a) * (a * a + 1.0)
    return a, jnp.sqrt(one_minus_a2) * (i_g * u)


def _lru_seq_kernel(xr_ref, gr_ref, cw_ref, cb_ref, wg_ref, bg_ref, lam_ref,
                    y_ref, h_ref, tail_ref, xbuf, hbuf):
    t = T_LRU
    j = pl.program_id(1)

    @pl.when(j == 0)
    def _():
        xbuf[0:SUBLANES, :] = jnp.zeros((SUBLANES, D_LRU), F32)
        hbuf[...] = jnp.zeros(hbuf.shape, F32)

    x = xr_ref[...]
    xbuf[SUBLANES:SUBLANES + t, :] = x
    cw = cw_ref[...]
    u = cb_ref[...] + cw[CONV_WIDTH - 1:CONV_WIDTH] * x
    for d in range(1, CONV_WIDTH):
        u = u + cw[CONV_WIDTH - 1 - d:CONV_WIDTH - d] * xbuf[SUBLANES - d:SUBLANES - d + t, :]

    a, b = _lru_gates(u, wg_ref, bg_ref, lam_ref)

    groups = t // SUBLANES
    a = a.reshape(groups, SUBLANES, D_LRU)
    b = b.reshape(groups, SUBLANES, D_LRU)
    row = lax.broadcasted_iota(jnp.int32, a.shape, 1)
    shift = 1
    while shift < SUBLANES:
        a_prev = pltpu.roll(a, shift, 1)
        b_prev = pltpu.roll(b, shift, 1)
        live = row >= shift
        b = jnp.where(live, a * b_prev + b, b)
        a = jnp.where(live, a * a_prev, a)
        shift *= 2

    h = hbuf[SUBLANES - 1:SUBLANES, :]
    rows = []
    for g in range(groups):
        rows.append(b[g] + a[g] * h)
        h = rows[-1][SUBLANES - 1:SUBLANES, :]
    hs = jnp.concatenate(rows, axis=0)
    y_ref[...] = (hs * jax.nn.gelu(gr_ref[...])).astype(y_ref.dtype)

    hbuf[...] = hs[t - SUBLANES:t]
    xbuf[0:SUBLANES, :] = x[t - SUBLANES:t]
    h_ref[...] = hs[t - SUBLANES:t]
    tail_ref[...] = x[t - SUBLANES:t]


def _lru_seq_call(xr, gr, conv_w, conv_b, wg_bf, bg, lam, name):
    b, s, _ = xr.shape
    tok = lambda i, j: (i, j, 0)
    const = lambda i, j: (0, 0)
    per_seq = lambda i, j: (i, 0, 0)
    return pl.pallas_call(
        _lru_seq_kernel,
        grid=(b, s // T_LRU),
        in_specs=[
            pl.BlockSpec((None, T_LRU, D_LRU), tok),
            pl.BlockSpec((None, T_LRU, D_LRU), tok),
            pl.BlockSpec((CONV_WIDTH, D_LRU), const),
            pl.BlockSpec((1, D_LRU), const),
            pl.BlockSpec((D_LRU, 2 * D_LRU), const),
            pl.BlockSpec((1, 2 * D_LRU), const),
            pl.BlockSpec((1, D_LRU), const),
        ],
        out_specs=[
            pl.BlockSpec((None, T_LRU, D_LRU), tok),
            pl.BlockSpec((None, SUBLANES, D_LRU), per_seq),
            pl.BlockSpec((None, SUBLANES, D_LRU), per_seq),
        ],
        out_shape=[
            jax.ShapeDtypeStruct((b, s, D_LRU), BF16),
            jax.ShapeDtypeStruct((b, SUBLANES, D_LRU), F32),
            jax.ShapeDtypeStruct((b, SUBLANES, D_LRU), F32),
        ],
        scratch_shapes=[
            pltpu.VMEM((SUBLANES + T_LRU, D_LRU), F32),
            pltpu.VMEM((SUBLANES, D_LRU), F32),
        ],
        compiler_params=_cparams(2),
        name=name,
    )(xr, gr, conv_w, conv_b, wg_bf, bg, lam)


def _lru_step_kernel(xr_ref, gr_ref, cbuf_ref, h0_ref, cw_ref, cb_ref, wg_ref, bg_ref, lam_ref,
                     y_ref, h_ref):
    x = xr_ref[...]
    cw = cw_ref[...]
    u = cb_ref[...] + cw[CONV_WIDTH - 1:CONV_WIDTH] * x
    for d in range(CONV_WIDTH - 1):
        u = u + cw[d:d + 1] * cbuf_ref[d]
    a, b = _lru_gates(u, wg_ref, bg_ref, lam_ref)
    h = a * h0_ref[...] + b
    h_ref[...] = h
    y_ref[...] = (h * jax.nn.gelu(gr_ref[...])).astype(y_ref.dtype)


def _lru_step_call(xr, gr, cbuf_t, h0, conv_w, conv_b, wg_bf, bg, lam, name):
    b = xr.shape[0]
    return pl.pallas_call(
        _lru_step_kernel,
        out_shape=[jax.ShapeDtypeStruct((b, D_LRU), BF16),
                   jax.ShapeDtypeStruct((b, D_LRU), F32)],
        compiler_params=pltpu.CompilerParams(vmem_limit_bytes=VMEM_LIMIT_BYTES),
        name=name,
    )(xr, gr, cbuf_t, h0, conv_w, conv_b, wg_bf, bg, lam)


def _outffn_kernel(x_ref, oa_ref, yl_ref, gtm_ref, shf_ref, scf_ref, gtf_ref,
                   gpm_ref, gpf_ref, gqf_ref, wo_ref, wi_ref, wf_ref, o_ref):
    d_ff = wf_ref.shape[0]
    mix = _dot(oa_ref[...], wo_ref[0:D_ATT, :]) + _dot(yl_ref[...], wo_ref[D_ATT:D_MODEL, :])
    x1 = x_ref[...] + gtm_ref[...] * _rms(mix, gpm_ref[...])
    h = (_rms(x1, gpf_ref[...]) * (1.0 + scf_ref[...]) + shf_ref[...]).astype(BF16)
    gate = _dot(h, wi_ref[:, 0:d_ff])
    up = _dot(h, wi_ref[:, d_ff:2 * d_ff])
    act = (gate * jax.nn.sigmoid(gate) * up).astype(BF16)
    f = _dot(act, wf_ref[...])
    o_ref[...] = x1 + gtf_ref[...] * _rms(f, gqf_ref[...])


def _outffn_call(x, oa, yl, gtm, shf, scf, gtf, gpm, gpf, gqf, wo_bf, wi_bf, wf_bf, tm, name):
    b, s, _ = x.shape
    r = gtm.shape[1]
    rb = 1 if r == 1 else tm
    d_ff = wf_bf.shape[0]
    tok = lambda i, j: (i, j, 0)
    mod = (lambda i, j: (i, 0, 0)) if r == 1 else tok
    const = lambda i, j: (0, 0)
    resident = pl.Buffered(1)
    return pl.pallas_call(
        _outffn_kernel,
        grid=(b, s // tm),
        in_specs=[
            pl.BlockSpec((None, tm, D_MODEL), tok),
            pl.BlockSpec((None, tm, D_ATT), tok),
            pl.BlockSpec((None, tm, D_LRU), tok),
            pl.BlockSpec((None, rb, D_MODEL), mod),
            pl.BlockSpec((None, rb, D_MODEL), mod),
            pl.BlockSpec((None, rb, D_MODEL), mod),
            pl.BlockSpec((None, rb, D_MODEL), mod),
            pl.BlockSpec((1, D_MODEL), const),
            pl.BlockSpec((1, D_MODEL), const),
            pl.BlockSpec((1, D_MODEL), const),
            pl.BlockSpec((D_MODEL, D_MODEL), const, pipeline_mode=resident),
            pl.BlockSpec((D_MODEL, 2 * d_ff), const, pipeline_mode=resident),
            pl.BlockSpec((d_ff, D_MODEL), const, pipeline_mode=resident),
        ],
        out_specs=pl.BlockSpec((None, tm, D_MODEL), tok),
        out_shape=jax.ShapeDtypeStruct((b, s, D_MODEL), F32),
        compiler_params=_cparams(2),
        name=name,
    )(x, oa, yl, gtm, shf, scf, gtf, gpm, gpf, gqf, wo_bf, wi_bf, wf_bf)


def _block_diag(w):
    n, i, j = w.shape
    eye = jnp.eye(n, dtype=w.dtype)
    return (eye[:, None, :, None] * w[:, :, None, :]).reshape(n * i, n * j)


def kernel(x_prompt, x_sample, cache_k, cache_v, state_h, state_conv, page_table, c_prompt, c_sample,
           w_ada, b_ada, g_pre_mix, w_in, lam_q1, lam_k1, lam_q2, lam_k2, g_subln, conv_w, conv_b,
           lru_wa, lru_ba, lru_wx, lru_bx, lru_lambda, w_out, g_post_mix, g_pre_ffn, w_ffn_in,
           w_ffn_out, g_post_ffn):
    depth = w_ada.shape[0]
    bp, seq, _ = x_prompt.shape
    bs = x_sample.shape[0]
    page = cache_k.shape[2]
    past = page_table.shape[1] * page

    pad = (-(bs + bp)) % SUBLANES
    c_all = jnp.concatenate([c_sample, c_prompt, jnp.zeros((pad, D_MODEL), F32)], axis=0)
    mod = _ada_call(c_all, w_ada.astype(BF16), b_ada)

    ys = x_sample.reshape(1, bs, D_MODEL)
    yp = x_prompt

    slope_of = lambda h: 2.0 ** (-8.0 * (h + 1) / N_HEADS)
    slopes = jnp.asarray([slope_of(h) for h in range(N_HEADS)], F32)
    dec_bias = np.zeros((past, LANES), np.float32)
    for j in range(2 * N_HEADS):
        dec_bias[:, j] = -slope_of(j % N_HEADS) * (past - np.arange(past, dtype=np.float32))
    dec_bias = jnp.asarray(dec_bias)
    ck = cache_k.reshape(depth, cache_k.shape[1], page * N_HEADS, D_V)
    cv = cache_v.reshape(depth, cache_v.shape[1], page * N_HEADS, D_V)

    outs = {n: [] for n in ("kp", "vp", "hp", "cp", "ks", "vs", "hs", "cs")}
    for l in range(depth):
        lam_init = 0.8 - 0.6 * math.exp(-0.3 * l)
        lamv = jnp.stack([lam_q1[l], lam_k1[l], lam_q2[l], lam_k2[l]])
        g_sub = g_subln[l].reshape(1, D_V)
        w_in_bf = w_in[l].astype(BF16)
        wo_bf = w_out[l].astype(BF16)
        wi_bf = w_ffn_in[l].astype(BF16)
        wf_bf = w_ffn_out[l].astype(BF16)
        wg_bf = jnp.concatenate([_block_diag(lru_wa[l]), _block_diag(lru_wx[l])], axis=1).astype(BF16)
        bg = jnp.concatenate([lru_ba[l].reshape(1, D_LRU), lru_bx[l].reshape(1, D_LRU)], axis=1)
        lam_row = lru_lambda[l].reshape(1, D_LRU)
        cb_row = conv_b[l].reshape(1, D_LRU)
        row = lambda v: v[l].reshape(1, D_MODEL)

        mods = [mod[l, :, i * D_MODEL:(i + 1) * D_MODEL] for i in range(6)]
        mod_s = [m[:bs].reshape(1, bs, D_MODEL) for m in mods]
        mod_p = [m[bs:bs + bp].reshape(bp, 1, D_MODEL) for m in mods]

        qt, k4, kb, v4, vt, xr, gr = _inproj_call(yp, mod_p[0], mod_p[1], row(g_pre_mix), w_in_bf,
                                                  TM_PROJ, True, f"inproj_p{l}")
        oa = _attn_call(slopes, qt, kb, vt, lamv, g_sub, lam_init, f"attn_p{l}")
        yl, h8, x8 = _lru_seq_call(xr, gr, conv_w[l], cb_row, wg_bf, bg, lam_row, f"lru_p{l}")
        yp = _outffn_call(yp, oa, yl, mod_p[2], mod_p[3], mod_p[4], mod_p[5],
                          row(g_post_mix), row(g_pre_ffn), row(g_post_ffn),
                          wo_bf, wi_bf, wf_bf, TM_FFN, f"outffn_p{l}")
        outs["kp"].append(k4.reshape(bp, seq, N_HEADS, D_V))
        outs["vp"].append(v4.reshape(bp, seq, N_HEADS, D_V))
        outs["hp"].append(h8[:, SUBLANES - 1])
        outs["cp"].append(x8[:, SUBLANES - (CONV_WIDTH - 1):])

        q4, k4, v4, xr, gr = _inproj_call(ys, mod_s[0], mod_s[1], row(g_pre_mix), w_in_bf,
                                          bs, False, f"inproj_s{l}")
        per_token = lambda a: a.reshape(bs, N_HEADS, D_V)
        oa = _decode_attn_call(page_table, per_token(q4), per_token(k4), per_token(v4), dec_bias, lamv, g_sub,
                               ck, cv, l, lam_init, f"attn_s{l}")
        yl, h_new = _lru_step_call(xr[0], gr[0], jnp.swapaxes(state_conv[l], 0, 1), state_h[l],
                                   conv_w[l], cb_row, wg_bf, bg, lam_row, f"lru_s{l}")
        ys = _outffn_call(ys, oa.reshape(1, bs, D_ATT), yl.reshape(1, bs, D_LRU),
                          mod_s[2], mod_s[3], mod_s[4], mod_s[5],
                          row(g_post_mix), row(g_pre_ffn), row(g_post_ffn),
                          wo_bf, wi_bf, wf_bf, bs, f"outffn_s{l}")
        outs["ks"].append(k4.reshape(bs, 1, N_HEADS, D_V))
        outs["vs"].append(v4.reshape(bs, 1, N_HEADS, D_V))
        outs["hs"].append(h_new)
        outs["cs"].append(jnp.concatenate([state_conv[l][:, 1:], xr[0][:, None, :]], axis=1))

    st = {n: jnp.stack(v) for n, v in outs.items()}
    return (yp, ys.reshape(bs, 1, D_MODEL), st["kp"], st["vp"], st["hp"], st["cp"],
            st["ks"], st["vs"], st["hs"], st["cs"])
```

```python
import functools
import math

import numpy as np
import jax
import jax.numpy as jnp
from jax import lax
from jax.experimental import pallas as pl
from jax.experimental.pallas import tpu as pltpu

D_MODEL = 1024
N_HEADS = 4
D_QK = 64
D_V = 2 * D_QK
D_ATT = N_HEADS * D_V
D_LRU = D_MODEL - D_ATT
N_LRU_BLOCKS = 8
CONV_WIDTH = 4
LRU_C = 8.0
D_IN = 3 * D_ATT + 2 * D_LRU
NORM_EPS = 1e-6
QK_SCALE = D_QK ** -0.5

F32 = jnp.float32
BF16 = jnp.bfloat16

VMEM_LIMIT_BYTES = 56 * 1024 * 1024

SUBLANES = 8
BF16_ROWS = 16
LANES = 128

TM_PROJ = 512
TM_FFN = 512
TQ = TM_PROJ
ATTN_CHUNK = 512
T_LRU = 256
ADA_TN = 1536
BF16_EXACT_INT = 256


def _cparams(n_axes):
    return pltpu.CompilerParams(
        dimension_semantics=("arbitrary",) * n_axes,
        vmem_limit_bytes=VMEM_LIMIT_BYTES)


def _rms(x, g):
    return x * lax.rsqrt(jnp.mean(x * x, axis=-1, keepdims=True) + NORM_EPS) * g


def _dot(a, b):
    return jnp.dot(a, b, preferred_element_type=F32)


def _dot_nt(a, b):
    return lax.dot_general(a, b, (((1,), (1,)), ((), ())), preferred_element_type=F32)


def _ada_kernel(c_ref, w_ref, b_ref, o_ref):
    c = c_ref[...]
    h = (c * jax.nn.sigmoid(c)).astype(BF16)
    o_ref[...] = _dot(h, w_ref[...].astype(BF16)) + b_ref[...]


def _ada_call(c_all, w_ada, b_ada):
    depth, _, n = w_ada.shape
    rows = c_all.shape[0]
    return pl.pallas_call(
        _ada_kernel,
        grid=(depth, n // ADA_TN),
        in_specs=[
            pl.BlockSpec((rows, D_MODEL), lambda l, j: (0, 0)),
            pl.BlockSpec((None, D_MODEL, ADA_TN), lambda l, j: (l, 0, j)),
            pl.BlockSpec((None, 1, ADA_TN), lambda l, j: (l, 0, j)),
        ],
        out_specs=pl.BlockSpec((None, rows, ADA_TN), lambda l, j: (l, 0, j)),
        out_shape=jax.ShapeDtypeStruct((depth, rows, n), F32),
        compiler_params=_cparams(2),
        name="ada_mod",
    )(c_all, w_ada, b_ada.reshape(depth, 1, n))


def _store_heads(ref, x):
    tm = x.shape[0]
    for h in range(N_HEADS):
        ref[pl.ds(h, tm, stride=N_HEADS), :] = x[:, h * D_V:(h + 1) * D_V]


def _inproj_kernel(x_ref, sh_ref, sc_ref, g_ref, w_ref, *outs, prompt):
    x = x_ref[...]
    h = (_rms(x, g_ref[...]) * (1.0 + sc_ref[...]) + sh_ref[...]).astype(BF16)
    q = _dot(h, w_ref[:, 0:D_ATT]) * QK_SCALE
    k = _dot(h, w_ref[:, D_ATT:2 * D_ATT])
    v = _dot(h, w_ref[:, 2 * D_ATT:3 * D_ATT])
    if prompt:
        qt_ref, k4_ref, kb_ref, v4_ref, vt_ref, xr_ref, gr_ref = outs
        qt_ref[...] = q.T.astype(BF16)
        kb_ref[...] = k.astype(BF16)
        vt_ref[...] = v.T.astype(BF16)
    else:
        q4_ref, k4_ref, v4_ref, xr_ref, gr_ref = outs
        _store_heads(q4_ref, q)
    _store_heads(k4_ref, k)
    _store_heads(v4_ref, v)
    xr_ref[...] = _dot(h, w_ref[:, 3 * D_ATT:3 * D_ATT + D_LRU])
    gr_ref[...] = _dot(h, w_ref[:, 3 * D_ATT + D_LRU:D_IN])


def _inproj_call(x, sh, sc, g, w_bf, tm, prompt, name):
    b, s, _ = x.shape
    nt = s // tm
    r = sh.shape[1]
    rb = 1 if r == 1 else tm
    tok = lambda i, j: (i, j, 0)
    tok4 = lambda i, j: (i, j, 0, 0)
    mod = (lambda i, j: (i, 0, 0)) if r == 1 else tok
    heads = (jax.ShapeDtypeStruct((b, s * N_HEADS, D_V), F32),
             pl.BlockSpec((None, tm * N_HEADS, D_V), tok))
    flat32 = (jax.ShapeDtypeStruct((b, s, D_LRU), F32), pl.BlockSpec((None, tm, D_LRU), tok))
    flat16 = (jax.ShapeDtypeStruct((b, s, D_ATT), BF16), pl.BlockSpec((None, tm, D_ATT), tok))
    tiles_t = (jax.ShapeDtypeStruct((b, nt, D_ATT, tm), BF16),
               pl.BlockSpec((None, None, D_ATT, tm), tok4))
    if prompt:
        outs = [tiles_t, heads, flat16, heads, tiles_t, flat32, flat32]
    else:
        outs = [heads, heads, heads, flat32, flat32]
    return pl.pallas_call(
        functools.partial(_inproj_kernel, prompt=prompt),
        grid=(b, nt),
        in_specs=[
            pl.BlockSpec((None, tm, D_MODEL), tok),
            pl.BlockSpec((None, rb, D_MODEL), mod),
            pl.BlockSpec((None, rb, D_MODEL), mod),
            pl.BlockSpec((1, D_MODEL), lambda i, j: (0, 0)),
            pl.BlockSpec((D_MODEL, D_IN), lambda i, j: (0, 0)),
        ],
        out_specs=[o[1] for o in outs],
        out_shape=[o[0] for o in outs],
        compiler_params=_cparams(2),
        name=name,
    )(x, sh, sc, g, w_bf)


def _diff_lambda(lam_ref, lam_init):
    lv = lam_ref[...]
    s1 = jnp.sum(lv[0:1] * lv[1:2], axis=-1, keepdims=True)
    s2 = jnp.sum(lv[2:3] * lv[3:4], axis=-1, keepdims=True)
    return jnp.exp(s1) - jnp.exp(s2) + lam_init


def _alibi_terms(tq):
    off = np.arange(tq)
    lo = (off % BF16_EXACT_INT).astype(np.float32)
    hi = (off - off % BF16_EXACT_INT).astype(np.float32)
    kterm = np.zeros((N_HEADS, tq, LANES), np.float32)
    qterm = np.zeros((N_HEADS, LANES, 2 * tq), np.float32)
    for h in range(N_HEADS):
        slope = 2.0 ** (-8.0 * (h + 1) / N_HEADS)
        kterm[h, :, 0] = slope * lo
        kterm[h, :, 1] = slope * hi
        kterm[h, :, 2:4] = 1.0
        qterm[h, 0:2, :] = 1.0
        qterm[h, 2, :] = np.tile(-slope * lo, 2)
        qterm[h, 3, :] = np.tile(-slope * hi, 2)
    return jnp.asarray(kterm, BF16), jnp.asarray(qterm, BF16)


def _attn_kernel(slope_ref, qt_ref, k_ref, vt_ref, kterm_ref, qterm_ref, lam_ref, g_ref, o_ref,
                 m_sc, acc_sc, s_sc, *, lam_init):
    h = pl.program_id(1)
    qi = pl.program_id(2)
    slope = slope_ref[h]
    tq = TQ

    qt = qt_ref[...]
    zq = jnp.zeros((D_QK, tq), BF16)
    feat = jnp.concatenate(
        [jnp.concatenate([qt[0:D_QK], zq], axis=0), jnp.concatenate([zq, qt[D_QK:D_V]], axis=0)],
        axis=1)
    rhs = jnp.concatenate([feat, qterm_ref[...]], axis=0)
    k_terms = kterm_ref[...]

    ones = jnp.ones((BF16_ROWS, tq), BF16)

    m_sc[...] = jnp.full(m_sc.shape, -jnp.inf, F32)
    acc_sc[...] = jnp.zeros(acc_sc.shape, F32)

    n_chunks = 2 * tq // ATTN_CHUNK
    chunk = lambda ci: slice(ci * ATTN_CHUNK, (ci + 1) * ATTN_CHUNK)

    def scores(kj, ci):
        start = pl.multiple_of(kj * tq, tq)
        lhs = jnp.concatenate([k_ref[pl.ds(start, tq), :], k_terms], axis=1)
        return _dot(lhs, rhs[:, chunk(ci)])

    def block(kj, rd, masked, last):
        vt = jnp.concatenate([vt_ref[kj], ones], axis=0)
        off = slope * ((qi - kj) * tq).astype(F32)
        for ci in range(n_chunks):
            cs = chunk(ci)
            s_ci = s_sc[rd, ci]
            if masked:
                key = lax.broadcasted_iota(jnp.int32, s_ci.shape, 0)
                qry = (lax.broadcasted_iota(jnp.int32, s_ci.shape, 1) + ci * ATTN_CHUNK) & (tq - 1)
                s_ci = jnp.where(qry >= key, s_ci, -jnp.inf)
            m_prev = m_sc[:, cs]
            m_new = jnp.maximum(m_prev, jnp.max(s_ci, axis=0, keepdims=True) - off)
            p = jnp.exp(s_ci - (m_new + off)).astype(BF16)
            alpha = jnp.exp(m_prev - m_new)
            if not last:
                s_sc[1 - rd, ci] = scores(kj + 1, ci)
            acc_sc[:, cs] = alpha * acc_sc[:, cs] + _dot(vt, p)
            m_sc[:, cs] = m_new

    for ci in range(n_chunks):
        s_sc[0, ci] = scores(0, ci)

    def pair(i, carry):
        block(2 * i, 0, masked=False, last=False)
        block(2 * i + 1, 1, masked=False, last=False)
        return carry

    lax.fori_loop(0, lax.shift_right_logical(qi, 1), pair, 0)

    @pl.when((qi & 1) == 1)
    def _():
        block(qi - 1, 0, masked=False, last=False)
        block(qi, 1, masked=True, last=True)

    @pl.when((qi & 1) == 0)
    def _():
        block(qi, 0, masked=True, last=True)

    lam = _diff_lambda(lam_ref, lam_init)
    acc = acc_sc[...]
    o1 = acc[0:D_V, 0:tq] / acc[D_V:D_V + 1, 0:tq]
    o2 = acc[0:D_V, tq:2 * tq] / acc[D_V:D_V + 1, tq:2 * tq]
    o = o1 - lam * o2
    o = o * lax.rsqrt(jnp.mean(o * o, axis=0, keepdims=True) + NORM_EPS)
    o_ref[...] = (o.T * g_ref[...] * (1.0 - lam_init)).astype(o_ref.dtype)


def _attn_call(slopes, qt, k_bf, vt, lamv, g_subln, lam_init, name):
    b, nt = qt.shape[0], qt.shape[1]
    s = nt * TQ
    const = lambda i, h, j: (0, 0)
    return pl.pallas_call(
        functools.partial(_attn_kernel, lam_init=lam_init),
        grid=(b, N_HEADS, nt),
        in_specs=[
            pl.BlockSpec(memory_space=pltpu.SMEM),
            pl.BlockSpec((None, None, D_V, TQ), lambda i, h, j: (i, j, h, 0)),
            pl.BlockSpec((None, s, D_V), lambda i, h, j: (i, 0, h)),
            pl.BlockSpec((None, nt, D_V, TQ), lambda i, h, j: (i, 0, h, 0)),
            pl.BlockSpec((None, TQ, LANES), lambda i, h, j: (h, 0, 0)),
            pl.BlockSpec((None, LANES, 2 * TQ), lambda i, h, j: (h, 0, 0)),
            pl.BlockSpec((4, D_QK), const),
            pl.BlockSpec((1, D_V), const),
        ],
        out_specs=pl.BlockSpec((None, TQ, D_V), lambda i, h, j: (i, j, h)),
        out_shape=jax.ShapeDtypeStruct((b, s, D_ATT), BF16),
        scratch_shapes=[
            pltpu.VMEM((1, 2 * TQ), F32),
            pltpu.VMEM((D_V + BF16_ROWS, 2 * TQ), F32),
            pltpu.VMEM((2, 2 * TQ // ATTN_CHUNK, TQ, ATTN_CHUNK), F32),
        ],
        compiler_params=_cparams(3),
        name=name,
    )(slopes, qt, k_bf, vt, *_alibi_terms(TQ), lamv, g_subln)


def _dot_tn(a, b):
    return lax.dot_general(a, b, (((0,), (0,)), ((), ())), preferred_element_type=F32)


def _decode_one(q_ref, kn_ref, vn_ref, bias_ref, lam_ref, g_ref, k_pages, v_pages, o_ref, lam_init):
    page = k_pages[0].shape[0] // N_HEADS
    cols = BF16_ROWS

    def flat(ref):
        return jnp.concatenate([ref[h:h + 1, :] for h in range(N_HEADS)], axis=1)

    def dense(ref):
        return jnp.concatenate(
            [ref[pl.ds(h, page, stride=N_HEADS), :] for h in range(N_HEADS)], axis=1).astype(BF16)

    def first_row(x):
        r = lax.broadcasted_iota(jnp.int32, (BF16_ROWS, D_ATT), 0)
        return jnp.where(r == 0, jnp.broadcast_to(x, (BF16_ROWS, D_ATT)), 0.0).astype(BF16)

    r = lax.broadcasted_iota(jnp.int32, (LANES, D_ATT), 0)
    c = lax.broadcasted_iota(jnp.int32, (LANES, D_ATT), 1)
    col = ((c >> 6) & 1) * N_HEADS + (c >> 7)
    qt = jnp.where(r == col, jnp.broadcast_to(flat(q_ref), (LANES, D_ATT)), 0.0).astype(BF16)

    s = jnp.concatenate([_dot_nt(dense(kp), qt) for kp in k_pages], axis=0)
    s = s + bias_ref[...]
    s_new = _dot_nt(first_row(flat(kn_ref)), qt)
    s_new = jnp.where(lax.broadcasted_iota(jnp.int32, s_new.shape, 0) == 0, s_new, -jnp.inf)

    m = jnp.maximum(jnp.max(s, axis=0, keepdims=True), jnp.max(s_new, axis=0, keepdims=True))
    e = jnp.exp(s - m)
    e_new = jnp.exp(s_new - m)
    l = jnp.sum(e, axis=0, keepdims=True) + jnp.sum(e_new, axis=0, keepdims=True)
    p = (e / l)[:, 0:cols].astype(BF16)
    p_new = (e_new / l)[:, 0:cols].astype(BF16)

    acc = _dot_tn(p_new, first_row(flat(vn_ref)))
    for i, vp in enumerate(v_pages):
        acc = acc + _dot_tn(p[i * page:(i + 1) * page], dense(vp))

    lam = _diff_lambda(lam_ref, lam_init)
    for h in range(N_HEADS):
        o = (acc[h:h + 1, h * D_V:(h + 1) * D_V]
             - lam * acc[N_HEADS + h:N_HEADS + h + 1, h * D_V:(h + 1) * D_V])
        o_ref[:, h * D_V:(h + 1) * D_V] = (_rms(o, g_ref[...]) * (1.0 - lam_init)).astype(o_ref.dtype)


def _decode_attn_kernel(pt_ref, q_ref, kn_ref, vn_ref, bias_ref, lam_ref, g_ref, ck_hbm, cv_hbm,
                        o_ref, kbuf, vbuf, sem, *, layer, n_pages, lam_init):
    i = pl.program_id(0)
    n = pl.num_programs(0)

    def page_copies(seq, slot):
        out = []
        for j in range(n_pages):
            pg = pt_ref[seq, j]
            out.append(pltpu.make_async_copy(ck_hbm.at[layer, pg], kbuf.at[slot, j], sem.at[0, slot]))
            out.append(pltpu.make_async_copy(cv_hbm.at[layer, pg], vbuf.at[slot, j], sem.at[1, slot]))
        return out

    def start(seq, slot):
        for cp in page_copies(seq, slot):
            cp.start()

    def wait(seq, slot):
        for cp in page_copies(seq, slot):
            cp.wait()

    def compute(slot):
        _decode_one(q_ref.at[slot], kn_ref.at[slot], vn_ref.at[slot], bias_ref, lam_ref, g_ref,
                    [kbuf.at[slot, j] for j in range(n_pages)],
                    [vbuf.at[slot, j] for j in range(n_pages)], o_ref.at[slot], lam_init)

    @pl.when(i == 0)
    def _():
        start(0, 0)

    start(2 * i + 1, 1)
    wait(2 * i, 0)
    compute(0)

    @pl.when(i + 1 < n)
    def _():
        start(2 * i + 2, 0)

    wait(2 * i + 1, 1)
    compute(1)


def _decode_attn_call(page_table, q4, k4, v4, bias, lamv, g_subln, cache_k, cache_v,
                      layer, lam_init, name):
    b = q4.shape[0]
    n_pages = page_table.shape[1]
    slab = cache_k.shape[2]
    per_step = 2
    tok = lambda i, pt: (i, 0, 0)
    const = lambda i, pt: (0, 0)
    grid_spec = pltpu.PrefetchScalarGridSpec(
        num_scalar_prefetch=1,
        grid=(b // per_step,),
        in_specs=[
            pl.BlockSpec((per_step, N_HEADS, D_V), tok),
            pl.BlockSpec((per_step, N_HEADS, D_V), tok),
            pl.BlockSpec((per_step, N_HEADS, D_V), tok),
            pl.BlockSpec(bias.shape, const),
            pl.BlockSpec((4, D_QK), const),
            pl.BlockSpec((1, D_V), const),
            pl.BlockSpec(memory_space=pl.ANY),
            pl.BlockSpec(memory_space=pl.ANY),
        ],
        out_specs=pl.BlockSpec((per_step, 1, D_ATT), tok),
        scratch_shapes=[
            pltpu.VMEM((per_step, n_pages, slab, D_V), F32),
            pltpu.VMEM((per_step, n_pages, slab, D_V), F32),
            pltpu.SemaphoreType.DMA((2, per_step)),
        ],
    )
    return pl.pallas_call(
        functools.partial(_decode_attn_kernel, layer=layer, n_pages=n_pages, lam_init=lam_init),
        grid_spec=grid_spec,
        out_shape=jax.ShapeDtypeStruct((b, 1, D_ATT), BF16),
        compiler_params=_cparams(1),
        name=name,
    )(page_table, q4, k4, v4, bias, lamv, g_subln, cache_k, cache_v)


def _lru_gates(u, wg_ref, bg_ref, lam_ref):
    gates = jax.nn.sigmoid(_dot(u.astype(BF16), wg_ref[...]) + bg_ref[...])
    r = gates[:, :D_LRU]
    i_g = gates[:, D_LRU:]
    neg_lam = -lam_ref[...]
    softplus = jnp.maximum(neg_lam, 0.0) + jnp.log1p(jnp.exp(-jnp.abs(neg_lam)))
    log_a = -LRU_C * r * softplus
    a = jnp.exp(log_a)
    one_minus_a2 = -jnp.tanh(log_a) * (a * a + 1.0)
    return a, jnp.sqrt(one_minus_a2) * (i_g * u)


def _lru_seq_kernel(xr_ref, gr_ref, cw_ref, cb_ref, wg_ref, bg_ref, lam_ref,
                    y_ref, h_ref, tail_ref, xbuf, hbuf):
    t = T_LRU
    j = pl.program_id(1)

    @pl.when(j == 0)
    def _():
        xbuf[0:SUBLANES, :] = jnp.zeros((SUBLANES, D_LRU), F32)
        hbuf[...] = jnp.zeros(hbuf.shape, F32)

    x = xr_ref[...]
    xbuf[SUBLANES:SUBLANES + t, :] = x
    cw = cw_ref[...]
    u = cb_ref[...] + cw[CONV_WIDTH - 1:CONV_WIDTH] * x
    for d in range(1, CONV_WIDTH):
        u = u + cw[CONV_WIDTH - 1 - d:CONV_WIDTH - d] * xbuf[SUBLANES - d:SUBLANES - d + t, :]

    a, b = _lru_gates(u, wg_ref, bg_ref, lam_ref)

    groups = t // SUBLANES
    a = a.reshape(groups, SUBLANES, D_LRU)
    b = b.reshape(groups, SUBLANES, D_LRU)
    row = lax.broadcasted_iota(jnp.int32, a.shape, 1)
    shift = 1
    while shift < SUBLANES:
        a_prev = pltpu.roll(a, shift, 1)
        b_prev = pltpu.roll(b, shift, 1)
        live = row >= shift
        b = jnp.where(live, a * b_prev + b, b)
        a = jnp.where(live, a * a_prev, a)
        shift *= 2

    h = hbuf[SUBLANES - 1:SUBLANES, :]
    rows = []
    for g in range(groups):
        rows.append(b[g] + a[g] * h)
        h = rows[-1][SUBLANES - 1:SUBLANES, :]
    hs = jnp.concatenate(rows, axis=0)
    y_ref[...] = (hs * jax.nn.gelu(gr_ref[...])).astype(y_ref.dtype)

    hbuf[...] = hs[t - SUBLANES:t]
    xbuf[0:SUBLANES, :] = x[t - SUBLANES:t]
    h_ref[...] = hs[t - SUBLANES:t]
    tail_ref[...] = x[t - SUBLANES:t]


def _lru_seq_call(xr, gr, conv_w, conv_b, wg_bf, bg, lam, name):
    b, s, _ = xr.shape
    tok = lambda i, j: (i, j, 0)
    const = lambda i, j: (0, 0)
    per_seq = lambda i, j: (i, 0, 0)
    return pl.pallas_call(
        _lru_seq_kernel,
        grid=(b, s // T_LRU),
        in_specs=[
            pl.BlockSpec((None, T_LRU, D_LRU), tok),
            pl.BlockSpec((None, T_LRU, D_LRU), tok),
            pl.BlockSpec((CONV_WIDTH, D_LRU), const),
            pl.BlockSpec((1, D_LRU), const),
            pl.BlockSpec((D_LRU, 2 * D_LRU), const),
            pl.BlockSpec((1, 2 * D_LRU), const),
            pl.BlockSpec((1, D_LRU), const),
        ],
        out_specs=[
            pl.BlockSpec((None, T_LRU, D_LRU), tok),
            pl.BlockSpec((None, SUBLANES, D_LRU), per_seq),
            pl.BlockSpec((None, SUBLANES, D_LRU), per_seq),
        ],
        out_shape=[
            jax.ShapeDtypeStruct((b, s, D_LRU), BF16),
            jax.ShapeDtypeStruct((b, SUBLANES, D_LRU), F32),
            jax.ShapeDtypeStruct((b, SUBLANES, D_LRU), F32),
        ],
        scratch_shapes=[
            pltpu.VMEM((SUBLANES + T_LRU, D_LRU), F32),
            pltpu.VMEM((SUBLANES, D_LRU), F32),
        ],
        compiler_params=_cparams(2),
        name=name,
    )(xr, gr, conv_w, conv_b, wg_bf, bg, lam)


def _lru_step_kernel(xr_ref, gr_ref, cbuf_ref, h0_ref, cw_ref, cb_ref, wg_ref, bg_ref, lam_ref,
                     y_ref, h_ref):
    x = xr_ref[...]
    cw = cw_ref[...]
    u = cb_ref[...] + cw[CONV_WIDTH - 1:CONV_WIDTH] * x
    for d in range(CONV_WIDTH - 1):
        u = u + cw[d:d + 1] * cbuf_ref[d]
    a, b = _lru_gates(u, wg_ref, bg_ref, lam_ref)
    h = a * h0_ref[...] + b
    h_ref[...] = h
    y_ref[...] = (h * jax.nn.gelu(gr_ref[...])).astype(y_ref.dtype)


def _lru_step_call(xr, gr, cbuf_t, h0, conv_w, conv_b, wg_bf, bg, lam, name):
    b = xr.shape[0]
    return pl.pallas_call(
        _lru_step_kernel,
        out_shape=[jax.ShapeDtypeStruct((b, D_LRU), BF16),
                   jax.ShapeDtypeStruct((b, D_LRU), F32)],
        compiler_params=pltpu.CompilerParams(vmem_limit_bytes=VMEM_LIMIT_BYTES),
        name=name,
    )(xr, gr, cbuf_t, h0, conv_w, conv_b, wg_bf, bg, lam)


def _outffn_kernel(x_ref, oa_ref, yl_ref, gtm_ref, shf_ref, scf_ref, gtf_ref,
                   gpm_ref, gpf_ref, gqf_ref, wo_ref, wi_ref, wf_ref, o_ref):
    d_ff = wf_ref.shape[0]
    mix = _dot(oa_ref[...], wo_ref[0:D_ATT, :]) + _dot(yl_ref[...], wo_ref[D_ATT:D_MODEL, :])
    x1 = x_ref[...] + gtm_ref[...] * _rms(mix, gpm_ref[...])
    h = (_rms(x1, gpf_ref[...]) * (1.0 + scf_ref[...]) + shf_ref[...]).astype(BF16)
    gate = _dot(h, wi_ref[:, 0:d_ff])
    up = _dot(h, wi_ref[:, d_ff:2 * d_ff])
    act = (gate * jax.nn.sigmoid(gate) * up).astype(BF16)
    f = _dot(act, wf_ref[...])
    o_ref[...] = x1 + gtf_ref[...] * _rms(f, gqf_ref[...])


def _outffn_call(x, oa, yl, gtm, shf, scf, gtf, gpm, gpf, gqf, wo_bf, wi_bf, wf_bf, tm, name):
    b, s, _ = x.shape
    r = gtm.shape[1]
    rb = 1 if r == 1 else tm
    d_ff = wf_bf.shape[0]
    tok = lambda i, j: (i, j, 0)
    mod = (lambda i, j: (i, 0, 0)) if r == 1 else tok
    const = lambda i, j: (0, 0)
    resident = pl.Buffered(1)
    return pl.pallas_call(
        _outffn_kernel,
        grid=(b, s // tm),
        in_specs=[
            pl.BlockSpec((None, tm, D_MODEL), tok),
            pl.BlockSpec((None, tm, D_ATT), tok),
            pl.BlockSpec((None, tm, D_LRU), tok),
            pl.BlockSpec((None, rb, D_MODEL), mod),
            pl.BlockSpec((None, rb, D_MODEL), mod),
            pl.BlockSpec((None, rb, D_MODEL), mod),
            pl.BlockSpec((None, rb, D_MODEL), mod),
            pl.BlockSpec((1, D_MODEL), const),
            pl.BlockSpec((1, D_MODEL), const),
            pl.BlockSpec((1, D_MODEL), const),
            pl.BlockSpec((D_MODEL, D_MODEL), const, pipeline_mode=resident),
            pl.BlockSpec((D_MODEL, 2 * d_ff), const, pipeline_mode=resident),
            pl.BlockSpec((d_ff, D_MODEL), const, pipeline_mode=resident),
        ],
        out_specs=pl.BlockSpec((None, tm, D_MODEL), tok),
        out_shape=jax.ShapeDtypeStruct((b, s, D_MODEL), F32),
        compiler_params=_cparams(2),
        name=name,
    )(x, oa, yl, gtm, shf, scf, gtf, gpm, gpf, gqf, wo_bf, wi_bf, wf_bf)


def _block_diag(w):
    n, i, j = w.shape
    eye = jnp.eye(n, dtype=w.dtype)
    return (eye[:, None, :, None] * w[:, :, None, :]).reshape(n * i, n * j)


def kernel(x_prompt, x_sample, cache_k, cache_v, state_h, state_conv, page_table, c_prompt, c_sample,
           w_ada, b_ada, g_pre_mix, w_in, lam_q1, lam_k1, lam_q2, lam_k2, g_subln, conv_w, conv_b,
           lru_wa, lru_ba, lru_wx, lru_bx, lru_lambda, w_out, g_post_mix, g_pre_ffn, w_ffn_in,
           w_ffn_out, g_post_ffn):
    depth = w_ada.shape[0]
    bp, seq, _ = x_prompt.shape
    bs = x_sample.shape[0]
    page = cache_k.shape[2]
    past = page_table.shape[1] * page

    pad = (-(bs + bp)) % SUBLANES
    c_all = jnp.concatenate([c_sample, c_prompt, jnp.zeros((pad, D_MODEL), F32)], axis=0)
    mod = _ada_call(c_all, w_ada, b_ada)

    ys = x_sample.reshape(1, bs, D_MODEL)
    yp = x_prompt

    slope_of = lambda h: 2.0 ** (-8.0 * (h + 1) / N_HEADS)
    slopes = jnp.asarray([slope_of(h) for h in range(N_HEADS)], F32)
    dec_bias = np.zeros((past, LANES), np.float32)
    for j in range(2 * N_HEADS):
        dec_bias[:, j] = -slope_of(j % N_HEADS) * (past - np.arange(past, dtype=np.float32))
    dec_bias = jnp.asarray(dec_bias)
    ck = cache_k.reshape(depth, cache_k.shape[1], page * N_HEADS, D_V)
    cv = cache_v.reshape(depth, cache_v.shape[1], page * N_HEADS, D_V)

    outs = {n: [] for n in ("kp", "vp", "hp", "cp", "ks", "vs", "hs", "cs")}
    for l in range(depth):
        lam_init = 0.8 - 0.6 * math.exp(-0.3 * l)
        lamv = jnp.stack([lam_q1[l], lam_k1[l], lam_q2[l], lam_k2[l]])
        g_sub = g_subln[l].reshape(1, D_V)
        w_in_bf = w_in[l].astype(BF16)
        wo_bf = w_out[l].astype(BF16)
        wi_bf = w_ffn_in[l].astype(BF16)
        wf_bf = w_ffn_out[l].astype(BF16)
        wg_bf = jnp.concatenate([_block_diag(lru_wa[l]), _block_diag(lru_wx[l])], axis=1).astype(BF16)
        bg = jnp.concatenate([lru_ba[l].reshape(1, D_LRU), lru_bx[l].reshape(1, D_LRU)], axis=1)
        lam_row = lru_lambda[l].reshape(1, D_LRU)
        cb_row = conv_b[l].reshape(1, D_LRU)
        row = lambda v: v[l].reshape(1, D_MODEL)

        mods = [mod[l, :, i * D_MODEL:(i + 1) * D_MODEL] for i in range(6)]
        mod_s = [m[:bs].reshape(1, bs, D_MODEL) for m in mods]
        mod_p = [m[bs:bs + bp].reshape(bp, 1, D_MODEL) for m in mods]

        qt, k4, kb, v4, vt, xr, gr = _inproj_call(yp, mod_p[0], mod_p[1], row(g_pre_mix), w_in_bf,
                                                  TM_PROJ, True, f"inproj_p{l}")
        oa = _attn_call(slopes, qt, kb, vt, lamv, g_sub, lam_init, f"attn_p{l}")
        yl, h8, x8 = _lru_seq_call(xr, gr, conv_w[l], cb_row, wg_bf, bg, lam_row, f"lru_p{l}")
        yp = _outffn_call(yp, oa, yl, mod_p[2], mod_p[3], mod_p[4], mod_p[5],
                          row(g_post_mix), row(g_pre_ffn), row(g_post_ffn),
                          wo_bf, wi_bf, wf_bf, TM_FFN, f"outffn_p{l}")
        outs["kp"].append(k4.reshape(bp, seq, N_HEADS, D_V))
        outs["vp"].append(v4.reshape(bp, seq, N_HEADS, D_V))
        outs["hp"].append(h8[:, SUBLANES - 1])
        outs["cp"].append(x8[:, SUBLANES - (CONV_WIDTH - 1):])

        q4, k4, v4, xr, gr = _inproj_call(ys, mod_s[0], mod_s[1], row(g_pre_mix), w_in_bf,
                                          bs, False, f"inproj_s{l}")
        per_token = lambda a: a.reshape(bs, N_HEADS, D_V)
        oa = _decode_attn_call(page_table, per_token(q4), per_token(k4), per_token(v4), dec_bias, lamv, g_sub,
                               ck, cv, l, lam_init, f"attn_s{l}")
        yl, h_new = _lru_step_call(xr[0], gr[0], jnp.swapaxes(state_conv[l], 0, 1), state_h[l],
                                   conv_w[l], cb_row, wg_bf, bg, lam_row, f"lru_s{l}")
        ys = _outffn_call(ys, oa.reshape(1, bs, D_ATT), yl.reshape(1, bs, D_LRU),
                          mod_s[2], mod_s[3], mod_s[4], mod_s[5],
                          row(g_post_mix), row(g_pre_ffn), row(g_post_ffn),
                          wo_bf, wi_bf, wf_bf, bs, f"outffn_s{l}")
        outs["ks"].append(k4.reshape(bs, 1, N_HEADS, D_V))
        outs["vs"].append(v4.reshape(bs, 1, N_HEADS, D_V))
        outs["hs"].append(h_new)
        outs["cs"].append(jnp.concatenate([state_conv[l][:, 1:], xr[0][:, None, :]], axis=1))

    st = {n: jnp.stack(v) for n, v in outs.items()}
    return (yp, ys.reshape(bs, 1, D_MODEL), st["kp"], st["vp"], st["hp"], st["cp"],
            st["ks"], st["vs"], st["hs"], st["cs"])
```

```python
import functools
import math

import numpy as np
import jax
import jax.numpy as jnp
from jax import lax
from jax.experimental import pallas as pl
from jax.experimental.pallas import tpu as pltpu

D_MODEL = 1024
N_HEADS = 4
D_QK = 64
D_V = 2 * D_QK
D_ATT = N_HEADS * D_V
D_LRU = D_MODEL - D_ATT
N_LRU_BLOCKS = 8
CONV_WIDTH = 4
LRU_C = 8.0
D_IN = 3 * D_ATT + 2 * D_LRU
NORM_EPS = 1e-6
QK_SCALE = D_QK ** -0.5

F32 = jnp.float32
BF16 = jnp.bfloat16

VMEM_LIMIT_BYTES = 56 * 1024 * 1024

SUBLANES = 8
BF16_ROWS = 16
LANES = 128

TM_PROJ = 512
TM_FFN = 512
TQ = TM_PROJ
ATTN_CHUNK = 512
T_LRU = 256
DECODE_SLOTS = 4
DECODE_AHEAD = 2
ADA_TN = 1536
BF16_EXACT_INT = 256


def _cparams(n_axes):
    return pltpu.CompilerParams(
        dimension_semantics=("arbitrary",) * n_axes,
        vmem_limit_bytes=VMEM_LIMIT_BYTES)


def _rms(x, g):
    return x * lax.rsqrt(jnp.mean(x * x, axis=-1, keepdims=True) + NORM_EPS) * g


def _dot(a, b):
    return jnp.dot(a, b, preferred_element_type=F32)


def _dot_nt(a, b):
    return lax.dot_general(a, b, (((1,), (1,)), ((), ())), preferred_element_type=F32)


def _ada_kernel(c_ref, w_ref, b_ref, o_ref):
    c = c_ref[...]
    h = (c * jax.nn.sigmoid(c)).astype(BF16)
    o_ref[...] = _dot(h, w_ref[...].astype(BF16)) + b_ref[...]


def _ada_call(c_all, w_ada, b_ada):
    depth, _, n = w_ada.shape
    rows = c_all.shape[0]
    return pl.pallas_call(
        _ada_kernel,
        grid=(depth, n // ADA_TN),
        in_specs=[
            pl.BlockSpec((rows, D_MODEL), lambda l, j: (0, 0)),
            pl.BlockSpec((None, D_MODEL, ADA_TN), lambda l, j: (l, 0, j)),
            pl.BlockSpec((None, 1, ADA_TN), lambda l, j: (l, 0, j)),
        ],
        out_specs=pl.BlockSpec((None, rows, ADA_TN), lambda l, j: (l, 0, j)),
        out_shape=jax.ShapeDtypeStruct((depth, rows, n), F32),
        compiler_params=_cparams(2),
        name="ada_mod",
    )(c_all, w_ada, b_ada.reshape(depth, 1, n))


def _store_heads(ref, x):
    tm = x.shape[0]
    for h in range(N_HEADS):
        ref[pl.ds(h, tm, stride=N_HEADS), :] = x[:, h * D_V:(h + 1) * D_V]


def _inproj_kernel(x_ref, sh_ref, sc_ref, g_ref, w_ref, *rest, prompt, layer, depth):
    x = x_ref[...]
    h = (_rms(x, g_ref[...]) * (1.0 + sc_ref[...]) + sh_ref[...]).astype(BF16)
    q = _dot(h, w_ref[:, 0:D_ATT]) * QK_SCALE
    k = _dot(h, w_ref[:, D_ATT:2 * D_ATT])
    v = _dot(h, w_ref[:, 2 * D_ATT:3 * D_ATT])
    if prompt:
        qt_ref, k4_ref, kb_ref, v4_ref, vt_ref, xr_ref, gr_ref = rest[-7:]
        qt_ref[...] = q.T.astype(BF16)
        kb_ref[...] = k.astype(BF16)
        vt_ref[...] = v.T.astype(BF16)
        if layer == 0:
            for ref in (k4_ref, v4_ref):
                for d in range(1, depth):
                    ref[d] = jnp.zeros(ref.shape[1:], ref.dtype)
            k4_ref, v4_ref = k4_ref.at[0], v4_ref.at[0]
    else:
        q4_ref, k4_ref, v4_ref, xr_ref, gr_ref = rest
        _store_heads(q4_ref, q)
    _store_heads(k4_ref, k)
    _store_heads(v4_ref, v)
    xr_ref[...] = _dot(h, w_ref[:, 3 * D_ATT:3 * D_ATT + D_LRU])
    gr_ref[...] = _dot(h, w_ref[:, 3 * D_ATT + D_LRU:D_IN])


def _inproj_call(x, sh, sc, g, w_bf, tm, prompt, name, layer=0, depth=1, kv_stack=()):
    b, s, _ = x.shape
    nt = s // tm
    r = sh.shape[1]
    rb = 1 if r == 1 else tm
    tok = lambda i, j: (i, j, 0)
    tok4 = lambda i, j: (i, j, 0, 0)
    mod = (lambda i, j: (i, 0, 0)) if r == 1 else tok
    heads = (jax.ShapeDtypeStruct((b, s * N_HEADS, D_V), F32),
             pl.BlockSpec((None, tm * N_HEADS, D_V), tok))
    stack_shape = jax.ShapeDtypeStruct((depth, b, s * N_HEADS, D_V), F32)
    if layer == 0:
        stack = (stack_shape, pl.BlockSpec((depth, None, tm * N_HEADS, D_V), lambda i, j: (0, i, j, 0)))
    else:
        stack = (stack_shape, pl.BlockSpec((None, None, tm * N_HEADS, D_V), lambda i, j: (layer, i, j, 0)))
    flat32 = (jax.ShapeDtypeStruct((b, s, D_LRU), F32), pl.BlockSpec((None, tm, D_LRU), tok))
    flat16 = (jax.ShapeDtypeStruct((b, s, D_ATT), BF16), pl.BlockSpec((None, tm, D_ATT), tok))
    tiles_t = (jax.ShapeDtypeStruct((b, nt, D_ATT, tm), BF16),
               pl.BlockSpec((None, None, D_ATT, tm), tok4))
    if prompt:
        outs = [tiles_t, stack, flat16, stack, tiles_t, flat32, flat32]
    else:
        outs = [heads, heads, heads, flat32, flat32]
    n_in = 5
    aliases = {n_in + n: out for n, out in zip(range(len(kv_stack)), (1, 3))}
    return pl.pallas_call(
        functools.partial(_inproj_kernel, prompt=prompt, layer=layer, depth=depth),
        grid=(b, nt),
        in_specs=[
            pl.BlockSpec((None, tm, D_MODEL), tok),
            pl.BlockSpec((None, rb, D_MODEL), mod),
            pl.BlockSpec((None, rb, D_MODEL), mod),
            pl.BlockSpec((1, D_MODEL), lambda i, j: (0, 0)),
            pl.BlockSpec((D_MODEL, D_IN), lambda i, j: (0, 0)),
        ] + [pl.BlockSpec(memory_space=pl.ANY) for _ in kv_stack],
        out_specs=[o[1] for o in outs],
        out_shape=[o[0] for o in outs],
        input_output_aliases=aliases,
        compiler_params=_cparams(2),
        name=name,
    )(x, sh, sc, g, w_bf, *kv_stack)


def _diff_lambda(lam_ref, lam_init):
    lv = lam_ref[...]
    s1 = jnp.sum(lv[0:1] * lv[1:2], axis=-1, keepdims=True)
    s2 = jnp.sum(lv[2:3] * lv[3:4], axis=-1, keepdims=True)
    return jnp.exp(s1) - jnp.exp(s2) + lam_init


def _alibi_terms(tq):
    off = np.arange(tq)
    lo = (off % BF16_EXACT_INT).astype(np.float32)
    hi = (off - off % BF16_EXACT_INT).astype(np.float32)
    kterm = np.zeros((N_HEADS, tq, LANES), np.float32)
    qterm = np.zeros((N_HEADS, LANES, 2 * tq), np.float32)
    for h in range(N_HEADS):
        slope = 2.0 ** (-8.0 * (h + 1) / N_HEADS)
        kterm[h, :, 0] = slope * lo
        kterm[h, :, 1] = slope * hi
        kterm[h, :, 2:4] = 1.0
        qterm[h, 0:2, :] = 1.0
        qterm[h, 2, :] = np.tile(-slope * lo, 2)
        qterm[h, 3, :] = np.tile(-slope * hi, 2)
    return jnp.asarray(kterm, BF16), jnp.asarray(qterm, BF16)


def _attn_kernel(slope_ref, qt_ref, k_ref, vt_ref, kterm_ref, qterm_ref, lam_ref, g_ref, o_ref,
                 m_sc, acc_sc, s_sc, *, lam_init):
    h = pl.program_id(1)
    qi = pl.program_id(2)
    slope = slope_ref[h]
    tq = TQ

    qt = qt_ref[...]
    zq = jnp.zeros((D_QK, tq), BF16)
    feat = jnp.concatenate(
        [jnp.concatenate([qt[0:D_QK], zq], axis=0), jnp.concatenate([zq, qt[D_QK:D_V]], axis=0)],
        axis=1)
    rhs = jnp.concatenate([feat, qterm_ref[...]], axis=0)
    k_terms = kterm_ref[...]

    ones = jnp.ones((BF16_ROWS, tq), BF16)

    m_sc[...] = jnp.full(m_sc.shape, -jnp.inf, F32)
    acc_sc[...] = jnp.zeros(acc_sc.shape, F32)

    n_chunks = 2 * tq // ATTN_CHUNK
    chunk = lambda ci: slice(ci * ATTN_CHUNK, (ci + 1) * ATTN_CHUNK)

    def scores(kj, ci):
        start = pl.multiple_of(kj * tq, tq)
        lhs = jnp.concatenate([k_ref[pl.ds(start, tq), :], k_terms], axis=1)
        return _dot(lhs, rhs[:, chunk(ci)])

    def block(kj, rd, masked, last):
        vt = jnp.concatenate([vt_ref[kj], ones], axis=0)
        off = slope * ((qi - kj) * tq).astype(F32)
        for ci in range(n_chunks):
            cs = chunk(ci)
            s_ci = s_sc[rd, ci]
            if masked:
                key = lax.broadcasted_iota(jnp.int32, s_ci.shape, 0)
                qry = (lax.broadcasted_iota(jnp.int32, s_ci.shape, 1) + ci * ATTN_CHUNK) & (tq - 1)
                s_ci = jnp.where(qry >= key, s_ci, -jnp.inf)
            m_prev = m_sc[:, cs]
            m_new = jnp.maximum(m_prev, jnp.max(s_ci, axis=0, keepdims=True) - off)
            p = jnp.exp(s_ci - (m_new + off)).astype(BF16)
            alpha = jnp.exp(m_prev - m_new)
            if not last:
                s_sc[1 - rd, ci] = scores(kj + 1, ci)
            acc_sc[:, cs] = alpha * acc_sc[:, cs] + _dot(vt, p)
            m_sc[:, cs] = m_new

    for ci in range(n_chunks):
        s_sc[0, ci] = scores(0, ci)

    def pair(i, carry):
        block(2 * i, 0, masked=False, last=False)
        block(2 * i + 1, 1, masked=False, last=False)
        return carry

    lax.fori_loop(0, lax.shift_right_logical(qi, 1), pair, 0)

    @pl.when((qi & 1) == 1)
    def _():
        block(qi - 1, 0, masked=False, last=False)
        block(qi, 1, masked=True, last=True)

    @pl.when((qi & 1) == 0)
    def _():
        block(qi, 0, masked=True, last=True)

    lam = _diff_lambda(lam_ref, lam_init)
    acc = acc_sc[...]
    o1 = acc[0:D_V, 0:tq] / acc[D_V:D_V + 1, 0:tq]
    o2 = acc[0:D_V, tq:2 * tq] / acc[D_V:D_V + 1, tq:2 * tq]
    o = o1 - lam * o2
    o = o * lax.rsqrt(jnp.mean(o * o, axis=0, keepdims=True) + NORM_EPS)
    o_ref[...] = (o.T * g_ref[...] * (1.0 - lam_init)).astype(o_ref.dtype)


def _attn_call(slopes, qt, k_bf, vt, lamv, g_subln, lam_init, name):
    b, nt = qt.shape[0], qt.shape[1]
    s = nt * TQ
    const = lambda i, h, j: (0, 0)
    return pl.pallas_call(
        functools.partial(_attn_kernel, lam_init=lam_init),
        grid=(b, N_HEADS, nt),
        in_specs=[
            pl.BlockSpec(memory_space=pltpu.SMEM),
            pl.BlockSpec((None, None, D_V, TQ), lambda i, h, j: (i, j, h, 0)),
            pl.BlockSpec((None, s, D_V), lambda i, h, j: (i, 0, h)),
            pl.BlockSpec((None, nt, D_V, TQ), lambda i, h, j: (i, 0, h, 0)),
            pl.BlockSpec((None, TQ, LANES), lambda i, h, j: (h, 0, 0)),
            pl.BlockSpec((None, LANES, 2 * TQ), lambda i, h, j: (h, 0, 0)),
            pl.BlockSpec((4, D_QK), const),
            pl.BlockSpec((1, D_V), const),
        ],
        out_specs=pl.BlockSpec((None, TQ, D_V), lambda i, h, j: (i, j, h)),
        out_shape=jax.ShapeDtypeStruct((b, s, D_ATT), BF16),
        scratch_shapes=[
            pltpu.VMEM((1, 2 * TQ), F32),
            pltpu.VMEM((D_V + BF16_ROWS, 2 * TQ), F32),
            pltpu.VMEM((2, 2 * TQ // ATTN_CHUNK, TQ, ATTN_CHUNK), F32),
        ],
        compiler_params=_cparams(3),
        name=name,
    )(slopes, qt, k_bf, vt, *_alibi_terms(TQ), lamv, g_subln)


def _dot_tn(a, b):
    return lax.dot_general(a, b, (((0,), (0,)), ((), ())), preferred_element_type=F32)


def _decode_one(q_ref, kn_ref, vn_ref, bias_ref, lam_ref, g_ref, k_pages, v_pages, o_ref, lam_init):
    page = k_pages[0].shape[0] // N_HEADS
    cols = BF16_ROWS

    def flat(ref):
        return jnp.concatenate([ref[h:h + 1, :] for h in range(N_HEADS)], axis=1)

    def dense(ref):
        return jnp.concatenate(
            [ref[pl.ds(h, page, stride=N_HEADS), :] for h in range(N_HEADS)], axis=1).astype(BF16)

    def first_row(x):
        r = lax.broadcasted_iota(jnp.int32, (BF16_ROWS, D_ATT), 0)
        return jnp.where(r == 0, jnp.broadcast_to(x, (BF16_ROWS, D_ATT)), 0.0).astype(BF16)

    r = lax.broadcasted_iota(jnp.int32, (LANES, D_ATT), 0)
    c = lax.broadcasted_iota(jnp.int32, (LANES, D_ATT), 1)
    col = ((c >> 6) & 1) * N_HEADS + (c >> 7)
    qt = jnp.where(r == col, jnp.broadcast_to(flat(q_ref), (LANES, D_ATT)), 0.0).astype(BF16)

    s = jnp.concatenate([_dot_nt(dense(kp), qt) for kp in k_pages], axis=0)
    s = s + bias_ref[...]
    s_new = _dot_nt(first_row(flat(kn_ref)), qt)
    s_new = jnp.where(lax.broadcasted_iota(jnp.int32, s_new.shape, 0) == 0, s_new, -jnp.inf)

    m = jnp.maximum(jnp.max(s, axis=0, keepdims=True), jnp.max(s_new, axis=0, keepdims=True))
    e = jnp.exp(s - m)
    e_new = jnp.exp(s_new - m)
    l = jnp.sum(e, axis=0, keepdims=True) + jnp.sum(e_new, axis=0, keepdims=True)
    p = (e / l)[:, 0:cols].astype(BF16)
    p_new = (e_new / l)[:, 0:cols].astype(BF16)

    acc = _dot_tn(p_new, first_row(flat(vn_ref)))
    for i, vp in enumerate(v_pages):
        acc = acc + _dot_tn(p[i * page:(i + 1) * page], dense(vp))

    lam = _diff_lambda(lam_ref, lam_init)
    for h in range(N_HEADS):
        o = (acc[h:h + 1, h * D_V:(h + 1) * D_V]
             - lam * acc[N_HEADS + h:N_HEADS + h + 1, h * D_V:(h + 1) * D_V])
        o_ref[:, h * D_V:(h + 1) * D_V] = (_rms(o, g_ref[...]) * (1.0 - lam_init)).astype(o_ref.dtype)


def _decode_attn_kernel(pt_ref, q_ref, kn_ref, vn_ref, bias_ref, lam_ref, g_ref, ck_hbm, cv_hbm,
                        o_ref, kbuf, vbuf, sem, *, layer, n_pages, lam_init):
    i = pl.program_id(0)
    n = pl.num_programs(0)

    def page_copies(seq, slot):
        out = []
        for j in range(n_pages):
            pg = pt_ref[seq, j]
            out.append(pltpu.make_async_copy(ck_hbm.at[layer, pg], kbuf.at[slot, j], sem.at[0, slot]))
            out.append(pltpu.make_async_copy(cv_hbm.at[layer, pg], vbuf.at[slot, j], sem.at[1, slot]))
        return out

    def start(seq, slot):
        for cp in page_copies(seq, slot):
            cp.start()

    def wait(seq, slot):
        for cp in page_copies(seq, slot):
            cp.wait()

    def compute(slot):
        _decode_one(q_ref.at[slot], kn_ref.at[slot], vn_ref.at[slot], bias_ref, lam_ref, g_ref,
                    [kbuf.at[slot, j] for j in range(n_pages)],
                    [vbuf.at[slot, j] for j in range(n_pages)], o_ref.at[slot], lam_init)

    per_step = kbuf.shape[0]

    @pl.when(i == 0)
    def _():
        for s in range(DECODE_AHEAD):
            start(s, s)

    for k in range(per_step):
        nxt = k + DECODE_AHEAD
        if nxt < per_step:
            start(per_step * i + nxt, nxt)
        else:
            @pl.when(i + 1 < n)
            def _(nxt=nxt):
                start(per_step * i + nxt, nxt - per_step)
        wait(per_step * i + k, k)
        compute(k)


def _decode_attn_call(page_table, q4, k4, v4, bias, lamv, g_subln, cache_k, cache_v,
                      layer, lam_init, name):
    b = q4.shape[0]
    n_pages = page_table.shape[1]
    slab = cache_k.shape[2]
    per_step = DECODE_SLOTS
    tok = lambda i, pt: (i, 0, 0)
    const = lambda i, pt: (0, 0)
    grid_spec = pltpu.PrefetchScalarGridSpec(
        num_scalar_prefetch=1,
        grid=(b // per_step,),
        in_specs=[
            pl.BlockSpec((per_step, N_HEADS, D_V), tok),
            pl.BlockSpec((per_step, N_HEADS, D_V), tok),
            pl.BlockSpec((per_step, N_HEADS, D_V), tok),
            pl.BlockSpec(bias.shape, const),
            pl.BlockSpec((4, D_QK), const),
            pl.BlockSpec((1, D_V), const),
            pl.BlockSpec(memory_space=pl.ANY),
            pl.BlockSpec(memory_space=pl.ANY),
        ],
        out_specs=pl.BlockSpec((per_step, 1, D_ATT), tok),
        scratch_shapes=[
            pltpu.VMEM((per_step, n_pages, slab, D_V), F32),
            pltpu.VMEM((per_step, n_pages, slab, D_V), F32),
            pltpu.SemaphoreType.DMA((2, per_step)),
        ],
    )
    return pl.pallas_call(
        functools.partial(_decode_attn_kernel, layer=layer, n_pages=n_pages, lam_init=lam_init),
        grid_spec=grid_spec,
        out_shape=jax.ShapeDtypeStruct((b, 1, D_ATT), BF16),
        compiler_params=_cparams(1),
        name=name,
    )(page_table, q4, k4, v4, bias, lamv, g_subln, cache_k, cache_v)


def _lru_gates(u, wg_ref, bg_ref, lam_ref):
    gates = jax.nn.sigmoid(_dot(u.astype(BF16), wg_ref[...]) + bg_ref[...])
    r = gates[:, :D_LRU]
    i_g = gates[:, D_LRU:]
    neg_lam = -lam_ref[...]
    softplus = jnp.maximum(neg_lam, 0.0) + jnp.log1p(jnp.exp(-jnp.abs(neg_lam)))
    log_a = -LRU_C * r * softplus
    a = jnp.exp(log_a)
    one_minus_a2 = -jnp.tanh(log_a) * (a * a + 1.0)
    return a, jnp.sqrt(one_minus_a2) * (i_g * u)


def _lru_seq_kernel(xr_ref, gr_ref, cw_ref, cb_ref, wg_ref, bg_ref, lam_ref,
                    y_ref, h_ref, tail_ref, xbuf, hbuf):
    t = T_LRU
    j = pl.program_id(1)

    @pl.when(j == 0)
    def _():
        xbuf[0:SUBLANES, :] = jnp.zeros((SUBLANES, D_LRU), F32)
        hbuf[...] = jnp.zeros(hbuf.shape, F32)

    x = xr_ref[...]
    xbuf[SUBLANES:SUBLANES + t, :] = x
    cw = cw_ref[...]
    u = cb_ref[...] + cw[CONV_WIDTH - 1:CONV_WIDTH] * x
    for d in range(1, CONV_WIDTH):
        u = u + cw[CONV_WIDTH - 1 - d:CONV_WIDTH - d] * xbuf[SUBLANES - d:SUBLANES - d + t, :]

    a, b = _lru_gates(u, wg_ref, bg_ref, lam_ref)

    groups = t // SUBLANES
    a = a.reshape(groups, SUBLANES, D_LRU)
    b = b.reshape(groups, SUBLANES, D_LRU)
    row = lax.broadcasted_iota(jnp.int32, a.shape, 1)
    shift = 1
    while shift < SUBLANES:
        a_prev = pltpu.roll(a, shift, 1)
        b_prev = pltpu.roll(b, shift, 1)
        live = row >= shift
        b = jnp.where(live, a * b_prev + b, b)
        a = jnp.where(live, a * a_prev, a)
        shift *= 2

    h = hbuf[SUBLANES - 1:SUBLANES, :]
    rows = []
    for g in range(groups):
        rows.append(b[g] + a[g] * h)
        h = rows[-1][SUBLANES - 1:SUBLANES, :]
    hs = jnp.concatenate(rows, axis=0)
    y_ref[...] = (hs * jax.nn.gelu(gr_ref[...])).astype(y_ref.dtype)

    hbuf[...] = hs[t - SUBLANES:t]
    xbuf[0:SUBLANES, :] = x[t - SUBLANES:t]
    h_ref[...] = hs[t - SUBLANES:t]
    tail_ref[...] = x[t - SUBLANES:t]


def _lru_seq_call(xr, gr, conv_w, conv_b, wg_bf, bg, lam, name):
    b, s, _ = xr.shape
    tok = lambda i, j: (i, j, 0)
    const = lambda i, j: (0, 0)
    per_seq = lambda i, j: (i, 0, 0)
    return pl.pallas_call(
        _lru_seq_kernel,
        grid=(b, s // T_LRU),
        in_specs=[
            pl.BlockSpec((None, T_LRU, D_LRU), tok),
            pl.BlockSpec((None, T_LRU, D_LRU), tok),
            pl.BlockSpec((CONV_WIDTH, D_LRU), const),
            pl.BlockSpec((1, D_LRU), const),
            pl.BlockSpec((D_LRU, 2 * D_LRU), const),
            pl.BlockSpec((1, 2 * D_LRU), const),
            pl.BlockSpec((1, D_LRU), const),
        ],
        out_specs=[
            pl.BlockSpec((None, T_LRU, D_LRU), tok),
            pl.BlockSpec((None, SUBLANES, D_LRU), per_seq),
            pl.BlockSpec((None, SUBLANES, D_LRU), per_seq),
        ],
        out_shape=[
            jax.ShapeDtypeStruct((b, s, D_LRU), BF16),
            jax.ShapeDtypeStruct((b, SUBLANES, D_LRU), F32),
            jax.ShapeDtypeStruct((b, SUBLANES, D_LRU), F32),
        ],
        scratch_shapes=[
            pltpu.VMEM((SUBLANES + T_LRU, D_LRU), F32),
            pltpu.VMEM((SUBLANES, D_LRU), F32),
        ],
        compiler_params=_cparams(2),
        name=name,
    )(xr, gr, conv_w, conv_b, wg_bf, bg, lam)


def _lru_step_kernel(xr_ref, gr_ref, cbuf_ref, h0_ref, cw_ref, cb_ref, wg_ref, bg_ref, lam_ref,
                     y_ref, h_ref):
    x = xr_ref[...]
    cw = cw_ref[...]
    u = cb_ref[...] + cw[CONV_WIDTH - 1:CONV_WIDTH] * x
    for d in range(CONV_WIDTH - 1):
        u = u + cw[d:d + 1] * cbuf_ref[d]
    a, b = _lru_gates(u, wg_ref, bg_ref, lam_ref)
    h = a * h0_ref[...] + b
    h_ref[...] = h
    y_ref[...] = (h * jax.nn.gelu(gr_ref[...])).astype(y_ref.dtype)


def _lru_step_call(xr, gr, cbuf_t, h0, conv_w, conv_b, wg_bf, bg, lam, name):
    b = xr.shape[0]
    return pl.pallas_call(
        _lru_step_kernel,
        out_shape=[jax.ShapeDtypeStruct((b, D_LRU), BF16),
                   jax.ShapeDtypeStruct((b, D_LRU), F32)],
        compiler_params=pltpu.CompilerParams(vmem_limit_bytes=VMEM_LIMIT_BYTES),
        name=name,
    )(xr, gr, cbuf_t, h0, conv_w, conv_b, wg_bf, bg, lam)


def _outffn_kernel(x_ref, oa_ref, yl_ref, gtm_ref, shf_ref, scf_ref, gtf_ref,
                   gpm_ref, gpf_ref, gqf_ref, wo_ref, wi_ref, wf_ref, o_ref):
    d_ff = wf_ref.shape[0]
    mix = _dot(oa_ref[...], wo_ref[0:D_ATT, :]) + _dot(yl_ref[...], wo_ref[D_ATT:D_MODEL, :])
    x1 = x_ref[...] + gtm_ref[...] * _rms(mix, gpm_ref[...])
    h = (_rms(x1, gpf_ref[...]) * (1.0 + scf_ref[...]) + shf_ref[...]).astype(BF16)
    gate = _dot(h, wi_ref[:, 0:d_ff])
    up = _dot(h, wi_ref[:, d_ff:2 * d_ff])
    act = (gate * jax.nn.sigmoid(gate) * up).astype(BF16)
    f = _dot(act, wf_ref[...])
    o_ref[...] = x1 + gtf_ref[...] * _rms(f, gqf_ref[...])


def _outffn_call(x, oa, yl, gtm, shf, scf, gtf, gpm, gpf, gqf, wo_bf, wi_bf, wf_bf, tm, name):
    b, s, _ = x.shape
    r = gtm.shape[1]
    rb = 1 if r == 1 else tm
    d_ff = wf_bf.shape[0]
    tok = lambda i, j: (i, j, 0)
    mod = (lambda i, j: (i, 0, 0)) if r == 1 else tok
    const = lambda i, j: (0, 0)
    resident = pl.Buffered(1)
    return pl.pallas_call(
        _outffn_kernel,
        grid=(b, s // tm),
        in_specs=[
            pl.BlockSpec((None, tm, D_MODEL), tok),
            pl.BlockSpec((None, tm, D_ATT), tok),
            pl.BlockSpec((None, tm, D_LRU), tok),
            pl.BlockSpec((None, rb, D_MODEL), mod),
            pl.BlockSpec((None, rb, D_MODEL), mod),
            pl.BlockSpec((None, rb, D_MODEL), mod),
            pl.BlockSpec((None, rb, D_MODEL), mod),
            pl.BlockSpec((1, D_MODEL), const),
            pl.BlockSpec((1, D_MODEL), const),
            pl.BlockSpec((1, D_MODEL), const),
            pl.BlockSpec((D_MODEL, D_MODEL), const, pipeline_mode=resident),
            pl.BlockSpec((D_MODEL, 2 * d_ff), const, pipeline_mode=resident),
            pl.BlockSpec((d_ff, D_MODEL), const, pipeline_mode=resident),
        ],
        out_specs=pl.BlockSpec((None, tm, D_MODEL), tok),
        out_shape=jax.ShapeDtypeStruct((b, s, D_MODEL), F32),
        compiler_params=_cparams(2),
        name=name,
    )(x, oa, yl, gtm, shf, scf, gtf, gpm, gpf, gqf, wo_bf, wi_bf, wf_bf)


def _block_diag(w):
    n, i, j = w.shape
    eye = jnp.eye(n, dtype=w.dtype)
    return (eye[:, None, :, None] * w[:, :, None, :]).reshape(n * i, n * j)


def kernel(x_prompt, x_sample, cache_k, cache_v, state_h, state_conv, page_table, c_prompt, c_sample,
           w_ada, b_ada, g_pre_mix, w_in, lam_q1, lam_k1, lam_q2, lam_k2, g_subln, conv_w, conv_b,
           lru_wa, lru_ba, lru_wx, lru_bx, lru_lambda, w_out, g_post_mix, g_pre_ffn, w_ffn_in,
           w_ffn_out, g_post_ffn):
    depth = w_ada.shape[0]
    bp, seq, _ = x_prompt.shape
    bs = x_sample.shape[0]
    page = cache_k.shape[2]
    past = page_table.shape[1] * page

    pad = (-(bs + bp)) % SUBLANES
    c_all = jnp.concatenate([c_sample, c_prompt, jnp.zeros((pad, D_MODEL), F32)], axis=0)
    mod = _ada_call(c_all, w_ada, b_ada)

    ys = x_sample.reshape(1, bs, D_MODEL)
    yp = x_prompt

    slope_of = lambda h: 2.0 ** (-8.0 * (h + 1) / N_HEADS)
    slopes = jnp.asarray([slope_of(h) for h in range(N_HEADS)], F32)
    dec_bias = np.zeros((past, LANES), np.float32)
    for j in range(2 * N_HEADS):
        dec_bias[:, j] = -slope_of(j % N_HEADS) * (past - np.arange(past, dtype=np.float32))
    dec_bias = jnp.asarray(dec_bias)
    ck = cache_k.reshape(depth, cache_k.shape[1], page * N_HEADS, D_V)
    cv = cache_v.reshape(depth, cache_v.shape[1], page * N_HEADS, D_V)

    outs = {n: [] for n in ("hp", "cp", "ks", "vs", "hs", "cs")}
    kv_stack = ()
    for l in range(depth):
        lam_init = 0.8 - 0.6 * math.exp(-0.3 * l)
        lamv = jnp.stack([lam_q1[l], lam_k1[l], lam_q2[l], lam_k2[l]])
        g_sub = g_subln[l].reshape(1, D_V)
        w_in_bf = w_in[l].astype(BF16)
        wo_bf = w_out[l].astype(BF16)
        wi_bf = w_ffn_in[l].astype(BF16)
        wf_bf = w_ffn_out[l].astype(BF16)
        wg_bf = jnp.concatenate([_block_diag(lru_wa[l]), _block_diag(lru_wx[l])], axis=1).astype(BF16)
        bg = jnp.concatenate([lru_ba[l].reshape(1, D_LRU), lru_bx[l].reshape(1, D_LRU)], axis=1)
        lam_row = lru_lambda[l].reshape(1, D_LRU)
        cb_row = conv_b[l].reshape(1, D_LRU)
        row = lambda v: v[l].reshape(1, D_MODEL)

        mods = [mod[l, :, i * D_MODEL:(i + 1) * D_MODEL] for i in range(6)]
        mod_s = [m[:bs].reshape(1, bs, D_MODEL) for m in mods]
        mod_p = [m[bs:bs + bp].reshape(bp, 1, D_MODEL) for m in mods]

        qt, kp_all, kb, vp_all, vt, xr, gr = _inproj_call(
            yp, mod_p[0], mod_p[1], row(g_pre_mix), w_in_bf, TM_PROJ, True, f"inproj_p{l}",
            layer=l, depth=depth, kv_stack=kv_stack)
        kv_stack = (kp_all, vp_all)
        oa = _attn_call(slopes, qt, kb, vt, lamv, g_sub, lam_init, f"attn_p{l}")
        yl, h8, x8 = _lru_seq_call(xr, gr, conv_w[l], cb_row, wg_bf, bg, lam_row, f"lru_p{l}")
        yp = _outffn_call(yp, oa, yl, mod_p[2], mod_p[3], mod_p[4], mod_p[5],
                          row(g_post_mix), row(g_pre_ffn), row(g_post_ffn),
                          wo_bf, wi_bf, wf_bf, TM_FFN, f"outffn_p{l}")
        outs["hp"].append(h8[:, SUBLANES - 1])
        outs["cp"].append(x8[:, SUBLANES - (CONV_WIDTH - 1):])

        q4, k4, v4, xr, gr = _inproj_call(ys, mod_s[0], mod_s[1], row(g_pre_mix), w_in_bf,
                                          bs, False, f"inproj_s{l}")
        per_token = lambda a: a.reshape(bs, N_HEADS, D_V)
        oa = _decode_attn_call(page_table, per_token(q4), per_token(k4), per_token(v4), dec_bias, lamv, g_sub,
                               ck, cv, l, lam_init, f"attn_s{l}")
        yl, h_new = _lru_step_call(xr[0], gr[0], jnp.swapaxes(state_conv[l], 0, 1), state_h[l],
                                   conv_w[l], cb_row, wg_bf, bg, lam_row, f"lru_s{l}")
        ys = _outffn_call(ys, oa.reshape(1, bs, D_ATT), yl.reshape(1, bs, D_LRU),
                          mod_s[2], mod_s[3], mod_s[4], mod_s[5],
                          row(g_post_mix), row(g_pre_ffn), row(g_post_ffn),
                          wo_bf, wi_bf, wf_bf, bs, f"outffn_s{l}")
        outs["ks"].append(k4.reshape(bs, 1, N_HEADS, D_V))
        outs["vs"].append(v4.reshape(bs, 1, N_HEADS, D_V))
        outs["hs"].append(h_new)
        outs["cs"].append(jnp.concatenate([state_conv[l][:, 1:], xr[0][:, None, :]], axis=1))

    st = {n: jnp.stack(v) for n, v in outs.items()}
    kp_all, vp_all = (a.reshape(depth, bp, seq, N_HEADS, D_V) for a in kv_stack)
    return (yp, ys.reshape(bs, 1, D_MODEL), kp_all, vp_all, st["hp"], st["cp"],
            st["ks"], st["vs"], st["hs"], st["cs"])
```

```python
import functools
import math

import numpy as np
import jax
import jax.numpy as jnp
from jax import lax
from jax.experimental import pallas as pl
from jax.experimental.pallas import tpu as pltpu

D_MODEL = 1024
N_HEADS = 4
D_QK = 64
D_V = 2 * D_QK
D_ATT = N_HEADS * D_V
D_LRU = D_MODEL - D_ATT
N_LRU_BLOCKS = 8
CONV_WIDTH = 4
LRU_C = 8.0
D_IN = 3 * D_ATT + 2 * D_LRU
NORM_EPS = 1e-6
QK_SCALE = D_QK ** -0.5

F32 = jnp.float32
BF16 = jnp.bfloat16

VMEM_LIMIT_BYTES = 56 * 1024 * 1024

SUBLANES = 8
BF16_ROWS = 16
LANES = 128

TM_PROJ = 512
TM_FFN = 512
TQ = TM_PROJ
ATTN_CHUNK = 512
FFN_ROW_GROUPS = 2
T_LRU = 512
DECODE_SLOTS = 4
DECODE_AHEAD = 2
ADA_TN = 1536
BF16_EXACT_INT = 256


def _cparams(n_axes):
    return pltpu.CompilerParams(
        dimension_semantics=("arbitrary",) * n_axes,
        vmem_limit_bytes=VMEM_LIMIT_BYTES)


def _rms(x, g):
    return x * lax.rsqrt(jnp.mean(x * x, axis=-1, keepdims=True) + NORM_EPS) * g


def _dot(a, b):
    return jnp.dot(a, b, preferred_element_type=F32)


def _dot_nt(a, b):
    return lax.dot_general(a, b, (((1,), (1,)), ((), ())), preferred_element_type=F32)


def _ada_kernel(c_ref, w_ref, b_ref, o_ref):
    c = c_ref[...]
    h = (c * jax.nn.sigmoid(c)).astype(BF16)
    o_ref[...] = _dot(h, w_ref[...].astype(BF16)) + b_ref[...]


def _ada_call(c_all, w_ada, b_ada):
    depth, _, n = w_ada.shape
    rows = c_all.shape[0]
    return pl.pallas_call(
        _ada_kernel,
        grid=(depth, n // ADA_TN),
        in_specs=[
            pl.BlockSpec((rows, D_MODEL), lambda l, j: (0, 0)),
            pl.BlockSpec((None, D_MODEL, ADA_TN), lambda l, j: (l, 0, j)),
            pl.BlockSpec((None, 1, ADA_TN), lambda l, j: (l, 0, j)),
        ],
        out_specs=pl.BlockSpec((None, rows, ADA_TN), lambda l, j: (l, 0, j)),
        out_shape=jax.ShapeDtypeStruct((depth, rows, n), F32),
        compiler_params=_cparams(2),
        name="ada_mod",
    )(c_all, w_ada, b_ada.reshape(depth, 1, n))


def _store_heads(ref, x):
    tm = x.shape[0]
    for h in range(N_HEADS):
        ref[pl.ds(h, tm, stride=N_HEADS), :] = x[:, h * D_V:(h + 1) * D_V]


def _inproj_kernel(x_ref, sh_ref, sc_ref, g_ref, w_ref, *rest, prompt, layer, depth):
    x = x_ref[...]
    h = (_rms(x, g_ref[...]) * (1.0 + sc_ref[...]) + sh_ref[...]).astype(BF16)
    q = _dot(h, w_ref[:, 0:D_ATT]) * QK_SCALE
    k = _dot(h, w_ref[:, D_ATT:2 * D_ATT])
    v = _dot(h, w_ref[:, 2 * D_ATT:3 * D_ATT])
    if prompt:
        qt_ref, k4_ref, kb_ref, v4_ref, vt_ref, xr_ref, gr_ref = rest[-7:]
        qt_ref[...] = q.T.astype(BF16)
        kb_ref[...] = k.astype(BF16)
        vt_ref[...] = v.T.astype(BF16)
        if layer == 0:
            for ref in (k4_ref, v4_ref):
                for d in range(1, depth):
                    ref[d] = jnp.zeros(ref.shape[1:], ref.dtype)
            k4_ref, v4_ref = k4_ref.at[0], v4_ref.at[0]
    else:
        q4_ref, k4_ref, v4_ref, xr_ref, gr_ref = rest
        _store_heads(q4_ref, q)
    _store_heads(k4_ref, k)
    _store_heads(v4_ref, v)
    xr_ref[...] = _dot(h, w_ref[:, 3 * D_ATT:3 * D_ATT + D_LRU])
    gr_ref[...] = _dot(h, w_ref[:, 3 * D_ATT + D_LRU:D_IN])


def _inproj_call(x, sh, sc, g, w_bf, tm, prompt, name, layer=0, depth=1, kv_stack=()):
    b, s, _ = x.shape
    nt = s // tm
    r = sh.shape[1]
    rb = 1 if r == 1 else tm
    tok = lambda i, j: (i, j, 0)
    tok4 = lambda i, j: (i, j, 0, 0)
    mod = (lambda i, j: (i, 0, 0)) if r == 1 else tok
    heads = (jax.ShapeDtypeStruct((b, s * N_HEADS, D_V), F32),
             pl.BlockSpec((None, tm * N_HEADS, D_V), tok))
    stack_shape = jax.ShapeDtypeStruct((depth, b, s * N_HEADS, D_V), F32)
    if layer == 0:
        stack = (stack_shape, pl.BlockSpec((depth, None, tm * N_HEADS, D_V), lambda i, j: (0, i, j, 0)))
    else:
        stack = (stack_shape, pl.BlockSpec((None, None, tm * N_HEADS, D_V), lambda i, j: (layer, i, j, 0)))
    flat32 = (jax.ShapeDtypeStruct((b, s, D_LRU), F32), pl.BlockSpec((None, tm, D_LRU), tok))
    flat16 = (jax.ShapeDtypeStruct((b, s, D_ATT), BF16), pl.BlockSpec((None, tm, D_ATT), tok))
    tiles_t = (jax.ShapeDtypeStruct((b, nt, D_ATT, tm), BF16),
               pl.BlockSpec((None, None, D_ATT, tm), tok4))
    if prompt:
        outs = [tiles_t, stack, flat16, stack, tiles_t, flat32, flat32]
    else:
        outs = [heads, heads, heads, flat32, flat32]
    n_in = 5
    aliases = {n_in + n: out for n, out in zip(range(len(kv_stack)), (1, 3))}
    return pl.pallas_call(
        functools.partial(_inproj_kernel, prompt=prompt, layer=layer, depth=depth),
        grid=(b, nt),
        in_specs=[
            pl.BlockSpec((None, tm, D_MODEL), tok),
            pl.BlockSpec((None, rb, D_MODEL), mod),
            pl.BlockSpec((None, rb, D_MODEL), mod),
            pl.BlockSpec((1, D_MODEL), lambda i, j: (0, 0)),
            pl.BlockSpec((D_MODEL, D_IN), lambda i, j: (0, 0)),
        ] + [pl.BlockSpec(memory_space=pl.ANY) for _ in kv_stack],
        out_specs=[o[1] for o in outs],
        out_shape=[o[0] for o in outs],
        input_output_aliases=aliases,
        compiler_params=_cparams(2),
        name=name,
    )(x, sh, sc, g, w_bf, *kv_stack)


def _diff_lambda(lam_ref, lam_init):
    lv = lam_ref[...]
    s1 = jnp.sum(lv[0:1] * lv[1:2], axis=-1, keepdims=True)
    s2 = jnp.sum(lv[2:3] * lv[3:4], axis=-1, keepdims=True)
    return jnp.exp(s1) - jnp.exp(s2) + lam_init


def _alibi_terms(tq):
    off = np.arange(tq)
    lo = (off % BF16_EXACT_INT).astype(np.float32)
    hi = (off - off % BF16_EXACT_INT).astype(np.float32)
    kterm = np.zeros((N_HEADS, tq, LANES), np.float32)
    qterm = np.zeros((N_HEADS, LANES, 2 * tq), np.float32)
    for h in range(N_HEADS):
        slope = 2.0 ** (-8.0 * (h + 1) / N_HEADS)
        kterm[h, :, 0] = slope * lo
        kterm[h, :, 1] = slope * hi
        kterm[h, :, 2:4] = 1.0
        qterm[h, 0:2, :] = 1.0
        qterm[h, 2, :] = np.tile(-slope * lo, 2)
        qterm[h, 3, :] = np.tile(-slope * hi, 2)
    return jnp.asarray(kterm, BF16), jnp.asarray(qterm, BF16)


def _attn_kernel(slope_ref, qt_ref, k_ref, vt_ref, kterm_ref, qterm_ref, lam_ref, g_ref, o_ref,
                 m_sc, acc_sc, s_sc, *, lam_init):
    h = pl.program_id(1)
    qi = pl.program_id(2)
    slope = slope_ref[h]
    tq = TQ

    qt = qt_ref[...]
    zq = jnp.zeros((D_QK, tq), BF16)
    feat = jnp.concatenate(
        [jnp.concatenate([qt[0:D_QK], zq], axis=0), jnp.concatenate([zq, qt[D_QK:D_V]], axis=0)],
        axis=1)
    rhs = jnp.concatenate([feat, qterm_ref[...]], axis=0)
    k_terms = kterm_ref[...]

    ones = jnp.ones((BF16_ROWS, tq), BF16)

    m_sc[...] = jnp.full(m_sc.shape, -jnp.inf, F32)
    acc_sc[...] = jnp.zeros(acc_sc.shape, F32)

    n_chunks = 2 * tq // ATTN_CHUNK
    chunk = lambda ci: slice(ci * ATTN_CHUNK, (ci + 1) * ATTN_CHUNK)

    def scores(kj, ci):
        start = pl.multiple_of(kj * tq, tq)
        lhs = jnp.concatenate([k_ref[pl.ds(start, tq), :], k_terms], axis=1)
        return _dot(lhs, rhs[:, chunk(ci)])

    def block(kj, rd, masked, last):
        vt = jnp.concatenate([vt_ref[kj], ones], axis=0)
        off = slope * ((qi - kj) * tq).astype(F32)
        for ci in range(n_chunks):
            cs = chunk(ci)
            s_ci = s_sc[rd, ci]
            if masked:
                key = lax.broadcasted_iota(jnp.int32, s_ci.shape, 0)
                qry = (lax.broadcasted_iota(jnp.int32, s_ci.shape, 1) + ci * ATTN_CHUNK) & (tq - 1)
                s_ci = jnp.where(qry >= key, s_ci, -jnp.inf)
            m_prev = m_sc[:, cs]
            m_new = jnp.maximum(m_prev, jnp.max(s_ci, axis=0, keepdims=True) - off)
            p = jnp.exp(s_ci - (m_new + off)).astype(BF16)
            alpha = jnp.exp(m_prev - m_new)
            if not last:
                s_sc[1 - rd, ci] = scores(kj + 1, ci)
            acc_sc[:, cs] = alpha * acc_sc[:, cs] + _dot(vt, p)
            m_sc[:, cs] = m_new

    for ci in range(n_chunks):
        s_sc[0, ci] = scores(0, ci)

    def pair(i, carry):
        block(2 * i, 0, masked=False, last=False)
        block(2 * i + 1, 1, masked=False, last=False)
        return carry

    lax.fori_loop(0, lax.shift_right_logical(qi, 1), pair, 0)

    @pl.when((qi & 1) == 1)
    def _():
        block(qi - 1, 0, masked=False, last=False)
        block(qi, 1, masked=True, last=True)

    @pl.when((qi & 1) == 0)
    def _():
        block(qi, 0, masked=True, last=True)

    lam = _diff_lambda(lam_ref, lam_init)
    acc = acc_sc[...]
    o1 = acc[0:D_V, 0:tq] / acc[D_V:D_V + 1, 0:tq]
    o2 = acc[0:D_V, tq:2 * tq] / acc[D_V:D_V + 1, tq:2 * tq]
    o = o1 - lam * o2
    o = o * lax.rsqrt(jnp.mean(o * o, axis=0, keepdims=True) + NORM_EPS)
    o_ref[...] = (o.T * g_ref[...] * (1.0 - lam_init)).astype(o_ref.dtype)


def _attn_call(slopes, qt, k_bf, vt, lamv, g_subln, lam_init, name):
    b, nt = qt.shape[0], qt.shape[1]
    s = nt * TQ
    const = lambda i, h, j: (0, 0)
    return pl.pallas_call(
        functools.partial(_attn_kernel, lam_init=lam_init),
        grid=(b, N_HEADS, nt),
        in_specs=[
            pl.BlockSpec(memory_space=pltpu.SMEM),
            pl.BlockSpec((None, None, D_V, TQ), lambda i, h, j: (i, j, h, 0)),
            pl.BlockSpec((None, s, D_V), lambda i, h, j: (i, 0, h)),
            pl.BlockSpec((None, nt, D_V, TQ), lambda i, h, j: (i, 0, h, 0)),
            pl.BlockSpec((None, TQ, LANES), lambda i, h, j: (h, 0, 0)),
            pl.BlockSpec((None, LANES, 2 * TQ), lambda i, h, j: (h, 0, 0)),
            pl.BlockSpec((4, D_QK), const),
            pl.BlockSpec((1, D_V), const),
        ],
        out_specs=pl.BlockSpec((None, TQ, D_V), lambda i, h, j: (i, j, h)),
        out_shape=jax.ShapeDtypeStruct((b, s, D_ATT), BF16),
        scratch_shapes=[
            pltpu.VMEM((1, 2 * TQ), F32),
            pltpu.VMEM((D_V + BF16_ROWS, 2 * TQ), F32),
            pltpu.VMEM((2, 2 * TQ // ATTN_CHUNK, TQ, ATTN_CHUNK), F32),
        ],
        compiler_params=_cparams(3),
        name=name,
    )(slopes, qt, k_bf, vt, *_alibi_terms(TQ), lamv, g_subln)


def _dot_tn(a, b):
    return lax.dot_general(a, b, (((0,), (0,)), ((), ())), preferred_element_type=F32)


def _decode_one(q_ref, kn_ref, vn_ref, bias_ref, lam_ref, g_ref, k_pages, v_pages, o_ref, lam_init):
    page = k_pages[0].shape[0] // N_HEADS
    cols = BF16_ROWS

    def flat(ref):
        return jnp.concatenate([ref[h:h + 1, :] for h in range(N_HEADS)], axis=1)

    def dense(ref):
        return jnp.concatenate(
            [ref[pl.ds(h, page, stride=N_HEADS), :] for h in range(N_HEADS)], axis=1).astype(BF16)

    def first_row(x):
        r = lax.broadcasted_iota(jnp.int32, (BF16_ROWS, D_ATT), 0)
        return jnp.where(r == 0, jnp.broadcast_to(x, (BF16_ROWS, D_ATT)), 0.0).astype(BF16)

    r = lax.broadcasted_iota(jnp.int32, (LANES, D_ATT), 0)
    c = lax.broadcasted_iota(jnp.int32, (LANES, D_ATT), 1)
    col = ((c >> 6) & 1) * N_HEADS + (c >> 7)
    qt = jnp.where(r == col, jnp.broadcast_to(flat(q_ref), (LANES, D_ATT)), 0.0).astype(BF16)

    s = jnp.concatenate([_dot_nt(dense(kp), qt) for kp in k_pages], axis=0)
    s = s + bias_ref[...]
    s_new = _dot_nt(first_row(flat(kn_ref)), qt)
    s_new = jnp.where(lax.broadcasted_iota(jnp.int32, s_new.shape, 0) == 0, s_new, -jnp.inf)

    m = jnp.maximum(jnp.max(s, axis=0, keepdims=True), jnp.max(s_new, axis=0, keepdims=True))
    e = jnp.exp(s - m)
    e_new = jnp.exp(s_new - m)
    l = jnp.sum(e, axis=0, keepdims=True) + jnp.sum(e_new, axis=0, keepdims=True)
    p = (e / l)[:, 0:cols].astype(BF16)
    p_new = (e_new / l)[:, 0:cols].astype(BF16)

    acc = _dot_tn(p_new, first_row(flat(vn_ref)))
    for i, vp in enumerate(v_pages):
        acc = acc + _dot_tn(p[i * page:(i + 1) * page], dense(vp))

    lam = _diff_lambda(lam_ref, lam_init)
    for h in range(N_HEADS):
        o = (acc[h:h + 1, h * D_V:(h + 1) * D_V]
             - lam * acc[N_HEADS + h:N_HEADS + h + 1, h * D_V:(h + 1) * D_V])
        o_ref[:, h * D_V:(h + 1) * D_V] = (_rms(o, g_ref[...]) * (1.0 - lam_init)).astype(o_ref.dtype)


def _decode_attn_kernel(pt_ref, q_ref, kn_ref, vn_ref, bias_ref, lam_ref, g_ref, ck_hbm, cv_hbm,
                        o_ref, kbuf, vbuf, sem, *, layer, n_pages, lam_init):
    i = pl.program_id(0)
    n = pl.num_programs(0)

    def page_copies(seq, slot):
        out = []
        for j in range(n_pages):
            pg = pt_ref[seq, j]
            out.append(pltpu.make_async_copy(ck_hbm.at[layer, pg], kbuf.at[slot, j], sem.at[0, slot]))
            out.append(pltpu.make_async_copy(cv_hbm.at[layer, pg], vbuf.at[slot, j], sem.at[1, slot]))
        return out

    def start(seq, slot):
        for cp in page_copies(seq, slot):
            cp.start()

    def wait(seq, slot):
        for cp in page_copies(seq, slot):
            cp.wait()

    def compute(slot):
        _decode_one(q_ref.at[slot], kn_ref.at[slot], vn_ref.at[slot], bias_ref, lam_ref, g_ref,
                    [kbuf.at[slot, j] for j in range(n_pages)],
                    [vbuf.at[slot, j] for j in range(n_pages)], o_ref.at[slot], lam_init)

    per_step = kbuf.shape[0]

    @pl.when(i == 0)
    def _():
        for s in range(DECODE_AHEAD):
            start(s, s)

    for k in range(per_step):
        nxt = k + DECODE_AHEAD
        if nxt < per_step:
            start(per_step * i + nxt, nxt)
        else:
            @pl.when(i + 1 < n)
            def _(nxt=nxt):
                start(per_step * i + nxt, nxt - per_step)
        wait(per_step * i + k, k)
        compute(k)


def _decode_attn_call(page_table, q4, k4, v4, bias, lamv, g_subln, cache_k, cache_v,
                      layer, lam_init, name):
    b = q4.shape[0]
    n_pages = page_table.shape[1]
    slab = cache_k.shape[2]
    per_step = DECODE_SLOTS
    tok = lambda i, pt: (i, 0, 0)
    const = lambda i, pt: (0, 0)
    grid_spec = pltpu.PrefetchScalarGridSpec(
        num_scalar_prefetch=1,
        grid=(b // per_step,),
        in_specs=[
            pl.BlockSpec((per_step, N_HEADS, D_V), tok),
            pl.BlockSpec((per_step, N_HEADS, D_V), tok),
            pl.BlockSpec((per_step, N_HEADS, D_V), tok),
            pl.BlockSpec(bias.shape, const),
            pl.BlockSpec((4, D_QK), const),
            pl.BlockSpec((1, D_V), const),
            pl.BlockSpec(memory_space=pl.ANY),
            pl.BlockSpec(memory_space=pl.ANY),
        ],
        out_specs=pl.BlockSpec((per_step, 1, D_ATT), tok),
        scratch_shapes=[
            pltpu.VMEM((per_step, n_pages, slab, D_V), F32),
            pltpu.VMEM((per_step, n_pages, slab, D_V), F32),
            pltpu.SemaphoreType.DMA((2, per_step)),
        ],
    )
    return pl.pallas_call(
        functools.partial(_decode_attn_kernel, layer=layer, n_pages=n_pages, lam_init=lam_init),
        grid_spec=grid_spec,
        out_shape=jax.ShapeDtypeStruct((b, 1, D_ATT), BF16),
        compiler_params=_cparams(1),
        name=name,
    )(page_table, q4, k4, v4, bias, lamv, g_subln, cache_k, cache_v)


def _lru_gates(u, wg_ref, bg_ref, lam_ref):
    gates = jax.nn.sigmoid(_dot(u.astype(BF16), wg_ref[...]) + bg_ref[...])
    r = gates[:, :D_LRU]
    i_g = gates[:, D_LRU:]
    neg_lam = -lam_ref[...]
    softplus = jnp.maximum(neg_lam, 0.0) + jnp.log1p(jnp.exp(-jnp.abs(neg_lam)))
    log_a = -LRU_C * r * softplus
    a = jnp.exp(log_a)
    one_minus_a2 = -jnp.tanh(log_a) * (a * a + 1.0)
    root = jnp.where(one_minus_a2 > 0.0, one_minus_a2 * lax.rsqrt(one_minus_a2), 0.0)
    return a, root * (i_g * u)


def _lru_seq_kernel(xr_ref, gr_ref, cw_ref, cb_ref, wg_ref, bg_ref, lam_ref,
                    y_ref, h_ref, tail_ref, xbuf, hbuf):
    t = T_LRU
    j = pl.program_id(1)

    @pl.when(j == 0)
    def _():
        xbuf[0:SUBLANES, :] = jnp.zeros((SUBLANES, D_LRU), F32)
        hbuf[...] = jnp.zeros(hbuf.shape, F32)

    x = xr_ref[...]
    xbuf[SUBLANES:SUBLANES + t, :] = x
    cw = cw_ref[...]
    u = cb_ref[...] + cw[CONV_WIDTH - 1:CONV_WIDTH] * x
    for d in range(1, CONV_WIDTH):
        u = u + cw[CONV_WIDTH - 1 - d:CONV_WIDTH - d] * xbuf[SUBLANES - d:SUBLANES - d + t, :]

    a, b = _lru_gates(u, wg_ref, bg_ref, lam_ref)

    groups = t // SUBLANES
    a = a.reshape(groups, SUBLANES, D_LRU)
    b = b.reshape(groups, SUBLANES, D_LRU)
    row = lax.broadcasted_iota(jnp.int32, a.shape, 1)
    shift = 1
    while shift < SUBLANES:
        a_prev = pltpu.roll(a, shift, 1)
        b_prev = pltpu.roll(b, shift, 1)
        live = row >= shift
        b = jnp.where(live, a * b_prev + b, b)
        a = jnp.where(live, a * a_prev, a)
        shift *= 2

    h = hbuf[SUBLANES - 1:SUBLANES, :]
    rows = []
    for g in range(groups):
        rows.append(b[g] + a[g] * h)
        h = rows[-1][SUBLANES - 1:SUBLANES, :]
    hs = jnp.concatenate(rows, axis=0)
    y_ref[...] = (hs * jax.nn.gelu(gr_ref[...])).astype(y_ref.dtype)

    hbuf[...] = hs[t - SUBLANES:t]
    xbuf[0:SUBLANES, :] = x[t - SUBLANES:t]
    h_ref[...] = hs[t - SUBLANES:t]
    tail_ref[...] = x[t - SUBLANES:t]


def _lru_seq_call(xr, gr, conv_w, conv_b, wg_bf, bg, lam, name):
    b, s, _ = xr.shape
    tok = lambda i, j: (i, j, 0)
    const = lambda i, j: (0, 0)
    per_seq = lambda i, j: (i, 0, 0)
    return pl.pallas_call(
        _lru_seq_kernel,
        grid=(b, s // T_LRU),
        in_specs=[
            pl.BlockSpec((None, T_LRU, D_LRU), tok),
            pl.BlockSpec((None, T_LRU, D_LRU), tok),
            pl.BlockSpec((CONV_WIDTH, D_LRU), const),
            pl.BlockSpec((1, D_LRU), const),
            pl.BlockSpec((D_LRU, 2 * D_LRU), const),
            pl.BlockSpec((1, 2 * D_LRU), const),
            pl.BlockSpec((1, D_LRU), const),
        ],
        out_specs=[
            pl.BlockSpec((None, T_LRU, D_LRU), tok),
            pl.BlockSpec((None, SUBLANES, D_LRU), per_seq),
            pl.BlockSpec((None, SUBLANES, D_LRU), per_seq),
        ],
        out_shape=[
            jax.ShapeDtypeStruct((b, s, D_LRU), BF16),
            jax.ShapeDtypeStruct((b, SUBLANES, D_LRU), F32),
            jax.ShapeDtypeStruct((b, SUBLANES, D_LRU), F32),
        ],
        scratch_shapes=[
            pltpu.VMEM((SUBLANES + T_LRU, D_LRU), F32),
            pltpu.VMEM((SUBLANES, D_LRU), F32),
        ],
        compiler_params=_cparams(2),
        name=name,
    )(xr, gr, conv_w, conv_b, wg_bf, bg, lam)


def _lru_step_kernel(xr_ref, gr_ref, cbuf_ref, h0_ref, cw_ref, cb_ref, wg_ref, bg_ref, lam_ref,
                     y_ref, h_ref):
    x = xr_ref[...]
    cw = cw_ref[...]
    u = cb_ref[...] + cw[CONV_WIDTH - 1:CONV_WIDTH] * x
    for d in range(CONV_WIDTH - 1):
        u = u + cw[d:d + 1] * cbuf_ref[d]
    a, b = _lru_gates(u, wg_ref, bg_ref, lam_ref)
    h = a * h0_ref[...] + b
    h_ref[...] = h
    y_ref[...] = (h * jax.nn.gelu(gr_ref[...])).astype(y_ref.dtype)


def _lru_step_call(xr, gr, cbuf_t, h0, conv_w, conv_b, wg_bf, bg, lam, name):
    b = xr.shape[0]
    return pl.pallas_call(
        _lru_step_kernel,
        out_shape=[jax.ShapeDtypeStruct((b, D_LRU), BF16),
                   jax.ShapeDtypeStruct((b, D_LRU), F32)],
        compiler_params=pltpu.CompilerParams(vmem_limit_bytes=VMEM_LIMIT_BYTES),
        name=name,
    )(xr, gr, cbuf_t, h0, conv_w, conv_b, wg_bf, bg, lam)


def _outffn_kernel(x_ref, oa_ref, yl_ref, gtm_ref, shf_ref, scf_ref, gtf_ref,
                   gpm_ref, gpf_ref, gqf_ref, wo_ref, wi_ref, wf_ref, o_ref):
    d_ff = wf_ref.shape[0]
    tm = x_ref.shape[0]
    n_groups = FFN_ROW_GROUPS if tm % (FFN_ROW_GROUPS * BF16_ROWS) == 0 else 1
    rows = [pl.ds(g * (tm // n_groups), tm // n_groups) for g in range(n_groups)]
    per_tok = lambda ref, r: ref[r, :] if ref.shape[0] == tm else ref[...]

    mix = [_dot(oa_ref[r, :], wo_ref[0:D_ATT, :]) + _dot(yl_ref[r, :], wo_ref[D_ATT:D_MODEL, :])
           for r in rows]
    x1 = [x_ref[r, :] + per_tok(gtm_ref, r) * _rms(m, gpm_ref[...]) for r, m in zip(rows, mix)]
    h = [(_rms(x, gpf_ref[...]) * (1.0 + per_tok(scf_ref, r)) + per_tok(shf_ref, r)).astype(BF16)
         for r, x in zip(rows, x1)]
    act = []
    for hg in h:
        gate = _dot(hg, wi_ref[:, 0:d_ff])
        up = _dot(hg, wi_ref[:, d_ff:2 * d_ff])
        act.append((gate * jax.nn.sigmoid(gate) * up).astype(BF16))
    f = [_dot(a, wf_ref[...]) for a in act]
    for r, x, fg in zip(rows, x1, f):
        o_ref[r, :] = x + per_tok(gtf_ref, r) * _rms(fg, gqf_ref[...])


def _outffn_call(x, oa, yl, gtm, shf, scf, gtf, gpm, gpf, gqf, wo_bf, wi_bf, wf_bf, tm, name):
    b, s, _ = x.shape
    r = gtm.shape[1]
    rb = 1 if r == 1 else tm
    d_ff = wf_bf.shape[0]
    tok = lambda i, j: (i, j, 0)
    mod = (lambda i, j: (i, 0, 0)) if r == 1 else tok
    const = lambda i, j: (0, 0)
    resident = pl.Buffered(1)
    return pl.pallas_call(
        _outffn_kernel,
        grid=(b, s // tm),
        in_specs=[
            pl.BlockSpec((None, tm, D_MODEL), tok),
            pl.BlockSpec((None, tm, D_ATT), tok),
            pl.BlockSpec((None, tm, D_LRU), tok),
            pl.BlockSpec((None, rb, D_MODEL), mod),
            pl.BlockSpec((None, rb, D_MODEL), mod),
            pl.BlockSpec((None, rb, D_MODEL), mod),
            pl.BlockSpec((None, rb, D_MODEL), mod),
            pl.BlockSpec((1, D_MODEL), const),
            pl.BlockSpec((1, D_MODEL), const),
            pl.BlockSpec((1, D_MODEL), const),
            pl.BlockSpec((D_MODEL, D_MODEL), const, pipeline_mode=resident),
            pl.BlockSpec((D_MODEL, 2 * d_ff), const, pipeline_mode=resident),
            pl.BlockSpec((d_ff, D_MODEL), const, pipeline_mode=resident),
        ],
        out_specs=pl.BlockSpec((None, tm, D_MODEL), tok),
        out_shape=jax.ShapeDtypeStruct((b, s, D_MODEL), F32),
        compiler_params=_cparams(2),
        name=name,
    )(x, oa, yl, gtm, shf, scf, gtf, gpm, gpf, gqf, wo_bf, wi_bf, wf_bf)


def _block_diag(w):
    n, i, j = w.shape
    eye = jnp.eye(n, dtype=w.dtype)
    return (eye[:, None, :, None] * w[:, :, None, :]).reshape(n * i, n * j)


def kernel(x_prompt, x_sample, cache_k, cache_v, state_h, state_conv, page_table, c_prompt, c_sample,
           w_ada, b_ada, g_pre_mix, w_in, lam_q1, lam_k1, lam_q2, lam_k2, g_subln, conv_w, conv_b,
           lru_wa, lru_ba, lru_wx, lru_bx, lru_lambda, w_out, g_post_mix, g_pre_ffn, w_ffn_in,
           w_ffn_out, g_post_ffn):
    depth = w_ada.shape[0]
    bp, seq, _ = x_prompt.shape
    bs = x_sample.shape[0]
    page = cache_k.shape[2]
    past = page_table.shape[1] * page

    pad = (-(bs + bp)) % SUBLANES
    c_all = jnp.concatenate([c_sample, c_prompt, jnp.zeros((pad, D_MODEL), F32)], axis=0)
    mod = _ada_call(c_all, w_ada, b_ada)

    ys = x_sample.reshape(1, bs, D_MODEL)
    yp = x_prompt

    slope_of = lambda h: 2.0 ** (-8.0 * (h + 1) / N_HEADS)
    slopes = jnp.asarray([slope_of(h) for h in range(N_HEADS)], F32)
    dec_bias = np.zeros((past, LANES), np.float32)
    for j in range(2 * N_HEADS):
        dec_bias[:, j] = -slope_of(j % N_HEADS) * (past - np.arange(past, dtype=np.float32))
    dec_bias = jnp.asarray(dec_bias)
    ck = cache_k.reshape(depth, cache_k.shape[1], page * N_HEADS, D_V)
    cv = cache_v.reshape(depth, cache_v.shape[1], page * N_HEADS, D_V)

    outs = {n: [] for n in ("hp", "cp", "ks", "vs", "hs", "cs")}
    kv_stack = ()
    for l in range(depth):
        lam_init = 0.8 - 0.6 * math.exp(-0.3 * l)
        lamv = jnp.stack([lam_q1[l], lam_k1[l], lam_q2[l], lam_k2[l]])
        g_sub = g_subln[l].reshape(1, D_V)
        w_in_bf = w_in[l].astype(BF16)
        wo_bf = w_out[l].astype(BF16)
        wi_bf = w_ffn_in[l].astype(BF16)
        wf_bf = w_ffn_out[l].astype(BF16)
        wg_bf = jnp.concatenate([_block_diag(lru_wa[l]), _block_diag(lru_wx[l])], axis=1).astype(BF16)
        bg = jnp.concatenate([lru_ba[l].reshape(1, D_LRU), lru_bx[l].reshape(1, D_LRU)], axis=1)
        lam_row = lru_lambda[l].reshape(1, D_LRU)
        cb_row = conv_b[l].reshape(1, D_LRU)
        row = lambda v: v[l].reshape(1, D_MODEL)

        mods = [mod[l, :, i * D_MODEL:(i + 1) * D_MODEL] for i in range(6)]
        mod_s = [m[:bs].reshape(1, bs, D_MODEL) for m in mods]
        mod_p = [m[bs:bs + bp].reshape(bp, 1, D_MODEL) for m in mods]

        qt, kp_all, kb, vp_all, vt, xr, gr = _inproj_call(
            yp, mod_p[0], mod_p[1], row(g_pre_mix), w_in_bf, TM_PROJ, True, f"inproj_p{l}",
            layer=l, depth=depth, kv_stack=kv_stack)
        kv_stack = (kp_all, vp_all)
        oa = _attn_call(slopes, qt, kb, vt, lamv, g_sub, lam_init, f"attn_p{l}")
        yl, h8, x8 = _lru_seq_call(xr, gr, conv_w[l], cb_row, wg_bf, bg, lam_row, f"lru_p{l}")
        yp = _outffn_call(yp, oa, yl, mod_p[2], mod_p[3], mod_p[4], mod_p[5],
                          row(g_post_mix), row(g_pre_ffn), row(g_post_ffn),
                          wo_bf, wi_bf, wf_bf, TM_FFN, f"outffn_p{l}")
        outs["hp"].append(h8[:, SUBLANES - 1])
        outs["cp"].append(x8[:, SUBLANES - (CONV_WIDTH - 1):])

        q4, k4, v4, xr, gr = _inproj_call(ys, mod_s[0], mod_s[1], row(g_pre_mix), w_in_bf,
                                          bs, False, f"inproj_s{l}")
        per_token = lambda a: a.reshape(bs, N_HEADS, D_V)
        oa = _decode_attn_call(page_table, per_token(q4), per_token(k4), per_token(v4), dec_bias, lamv, g_sub,
                               ck, cv, l, lam_init, f"attn_s{l}")
        yl, h_new = _lru_step_call(xr[0], gr[0], jnp.swapaxes(state_conv[l], 0, 1), state_h[l],
                                   conv_w[l], cb_row, wg_bf, bg, lam_row, f"lru_s{l}")
        ys = _outffn_call(ys, oa.reshape(1, bs, D_ATT), yl.reshape(1, bs, D_LRU),
                          mod_s[2], mod_s[3], mod_s[4], mod_s[5],
                          row(g_post_mix), row(g_pre_ffn), row(g_post_ffn),
                          wo_bf, wi_bf, wf_bf, bs, f"outffn_s{l}")
        outs["ks"].append(k4.reshape(bs, 1, N_HEADS, D_V))
        outs["vs"].append(v4.reshape(bs, 1, N_HEADS, D_V))
        outs["hs"].append(h_new)
        outs["cs"].append(jnp.concatenate([state_conv[l][:, 1:], xr[0][:, None, :]], axis=1))

    st = {n: jnp.stack(v) for n, v in outs.items()}
    kp_all, vp_all = (a.reshape(depth, bp, seq, N_HEADS, D_V) for a in kv_stack)
    return (yp, ys.reshape(bs, 1, D_MODEL), kp_all, vp_all, st["hp"], st["cp"],
            st["ks"], st["vs"], st["hs"], st["cs"])
```

```python
import functools
import math

import numpy as np
import jax
import jax.numpy as jnp
from jax import lax
from jax.experimental import pallas as pl
from jax.experimental.pallas import tpu as pltpu

D_MODEL = 1024
N_HEADS = 4
D_QK = 64
D_V = 2 * D_QK
D_ATT = N_HEADS * D_V
D_LRU = D_MODEL - D_ATT
N_LRU_BLOCKS = 8
CONV_WIDTH = 4
LRU_C = 8.0
D_IN = 3 * D_ATT + 2 * D_LRU
NORM_EPS = 1e-6
QK_SCALE = D_QK ** -0.5

F32 = jnp.float32
BF16 = jnp.bfloat16

VMEM_LIMIT_BYTES = 56 * 1024 * 1024

SUBLANES = 8
BF16_ROWS = 16
LANES = 128

TM_PROJ = 512
TM_FFN = 512
TQ = TM_PROJ
ATTN_CHUNK = 512
FFN_ROW_GROUPS = 2
T_LRU = 512
DECODE_SLOTS = 4
DECODE_AHEAD = 2
ADA_TN = 1536
BF16_EXACT_INT = 256


def _cparams(n_axes):
    return pltpu.CompilerParams(
        dimension_semantics=("arbitrary",) * n_axes,
        vmem_limit_bytes=VMEM_LIMIT_BYTES)


def _rms(x, g):
    return x * lax.rsqrt(jnp.mean(x * x, axis=-1, keepdims=True) + NORM_EPS) * g


def _dot(a, b):
    return jnp.dot(a, b, preferred_element_type=F32)


def _dot_nt(a, b):
    return lax.dot_general(a, b, (((1,), (1,)), ((), ())), preferred_element_type=F32)


def _ada_kernel(c_ref, w_ref, b_ref, o_ref):
    c = c_ref[...]
    h = (c * jax.nn.sigmoid(c)).astype(BF16)
    o_ref[...] = _dot(h, w_ref[...].astype(BF16)) + b_ref[...]


def _ada_call(c_all, w_ada, b_ada):
    depth, _, n = w_ada.shape
    rows = c_all.shape[0]
    return pl.pallas_call(
        _ada_kernel,
        grid=(depth, n // ADA_TN),
        in_specs=[
            pl.BlockSpec((rows, D_MODEL), lambda l, j: (0, 0)),
            pl.BlockSpec((None, D_MODEL, ADA_TN), lambda l, j: (l, 0, j)),
            pl.BlockSpec((None, 1, ADA_TN), lambda l, j: (l, 0, j)),
        ],
        out_specs=pl.BlockSpec((None, rows, ADA_TN), lambda l, j: (l, 0, j)),
        out_shape=jax.ShapeDtypeStruct((depth, rows, n), F32),
        compiler_params=_cparams(2),
        name="ada_mod",
    )(c_all, w_ada, b_ada.reshape(depth, 1, n))


def _store_heads(ref, x):
    tm = x.shape[0]
    for h in range(N_HEADS):
        ref[pl.ds(h, tm, stride=N_HEADS), :] = x[:, h * D_V:(h + 1) * D_V]


def _inproj_kernel(x_ref, sh_ref, sc_ref, g_ref, w_ref, *rest, prompt, layer, depth):
    x = x_ref[...]
    h = (_rms(x, g_ref[...]) * (1.0 + sc_ref[...]) + sh_ref[...]).astype(BF16)
    q = _dot(h, w_ref[:, 0:D_ATT]) * QK_SCALE
    k = _dot(h, w_ref[:, D_ATT:2 * D_ATT])
    v = _dot(h, w_ref[:, 2 * D_ATT:3 * D_ATT])
    if prompt:
        qt_ref, k4_ref, kb_ref, v4_ref, vt_ref, xr_ref, gr_ref = rest[-7:]
        qt_ref[...] = q.T.astype(BF16)
        for hd in range(N_HEADS):
            kb_ref[hd] = k[:, hd * D_V:(hd + 1) * D_V].astype(BF16)
        vt_ref[...] = v.T.astype(BF16)
        if layer == 0:
            for ref in (k4_ref, v4_ref):
                for d in range(1, depth):
                    ref[d] = jnp.zeros(ref.shape[1:], ref.dtype)
            k4_ref, v4_ref = k4_ref.at[0], v4_ref.at[0]
    else:
        q4_ref, k4_ref, v4_ref, xr_ref, gr_ref = rest
        _store_heads(q4_ref, q)
    _store_heads(k4_ref, k)
    _store_heads(v4_ref, v)
    xr_ref[...] = _dot(h, w_ref[:, 3 * D_ATT:3 * D_ATT + D_LRU])
    gr_ref[...] = _dot(h, w_ref[:, 3 * D_ATT + D_LRU:D_IN])


def _inproj_call(x, sh, sc, g, w_bf, tm, prompt, name, layer=0, depth=1, kv_stack=()):
    b, s, _ = x.shape
    nt = s // tm
    r = sh.shape[1]
    rb = 1 if r == 1 else tm
    tok = lambda i, j: (i, j, 0)
    tok4 = lambda i, j: (i, j, 0, 0)
    mod = (lambda i, j: (i, 0, 0)) if r == 1 else tok
    heads = (jax.ShapeDtypeStruct((b, s * N_HEADS, D_V), F32),
             pl.BlockSpec((None, tm * N_HEADS, D_V), tok))
    stack_shape = jax.ShapeDtypeStruct((depth, b, s * N_HEADS, D_V), F32)
    if layer == 0:
        stack = (stack_shape, pl.BlockSpec((depth, None, tm * N_HEADS, D_V), lambda i, j: (0, i, j, 0)))
    else:
        stack = (stack_shape, pl.BlockSpec((None, None, tm * N_HEADS, D_V), lambda i, j: (layer, i, j, 0)))
    flat32 = (jax.ShapeDtypeStruct((b, s, D_LRU), F32), pl.BlockSpec((None, tm, D_LRU), tok))
    flat16 = (jax.ShapeDtypeStruct((b, N_HEADS, s, D_V), BF16),
              pl.BlockSpec((None, N_HEADS, tm, D_V), lambda i, j: (i, 0, j, 0)))
    tiles_t = (jax.ShapeDtypeStruct((b, nt, D_ATT, tm), BF16),
               pl.BlockSpec((None, None, D_ATT, tm), tok4))
    if prompt:
        outs = [tiles_t, stack, flat16, stack, tiles_t, flat32, flat32]
    else:
        outs = [heads, heads, heads, flat32, flat32]
    n_in = 5
    aliases = {n_in + n: out for n, out in zip(range(len(kv_stack)), (1, 3))}
    return pl.pallas_call(
        functools.partial(_inproj_kernel, prompt=prompt, layer=layer, depth=depth),
        grid=(b, nt),
        in_specs=[
            pl.BlockSpec((None, tm, D_MODEL), tok),
            pl.BlockSpec((None, rb, D_MODEL), mod),
            pl.BlockSpec((None, rb, D_MODEL), mod),
            pl.BlockSpec((1, D_MODEL), lambda i, j: (0, 0)),
            pl.BlockSpec((None, D_MODEL, D_IN), lambda i, j: (layer, 0, 0)),
        ] + [pl.BlockSpec(memory_space=pl.ANY) for _ in kv_stack],
        out_specs=[o[1] for o in outs],
        out_shape=[o[0] for o in outs],
        input_output_aliases=aliases,
        compiler_params=_cparams(2),
        name=name,
    )(x, sh, sc, g, w_bf, *kv_stack)


def _diff_lambda(lam_ref, lam_init):
    lv = lam_ref[...]
    s1 = jnp.sum(lv[0:1] * lv[1:2], axis=-1, keepdims=True)
    s2 = jnp.sum(lv[2:3] * lv[3:4], axis=-1, keepdims=True)
    return jnp.exp(s1) - jnp.exp(s2) + lam_init


def _alibi_terms(tq):
    off = np.arange(tq)
    lo = (off % BF16_EXACT_INT).astype(np.float32)
    hi = (off - off % BF16_EXACT_INT).astype(np.float32)
    kterm = np.zeros((N_HEADS, tq, LANES), np.float32)
    qterm = np.zeros((N_HEADS, LANES, 2 * tq), np.float32)
    for h in range(N_HEADS):
        slope = 2.0 ** (-8.0 * (h + 1) / N_HEADS)
        kterm[h, :, 0] = slope * lo
        kterm[h, :, 1] = slope * hi
        kterm[h, :, 2:4] = 1.0
        qterm[h, 0:2, :] = 1.0
        qterm[h, 2, :] = np.tile(-slope * lo, 2)
        qterm[h, 3, :] = np.tile(-slope * hi, 2)
    return jnp.asarray(kterm, BF16), jnp.asarray(qterm, BF16)


def _attn_kernel(slope_ref, qt_ref, k_ref, vt_ref, kterm_ref, qterm_ref, lam_ref, g_ref, o_ref,
                 m_sc, acc_sc, s_sc, *, lam_init):
    h = pl.program_id(1)
    qi = pl.program_id(2)
    slope = slope_ref[h]
    tq = TQ

    qt = qt_ref[...]
    zq = jnp.zeros((D_QK, tq), BF16)
    feat = jnp.concatenate(
        [jnp.concatenate([qt[0:D_QK], zq], axis=0), jnp.concatenate([zq, qt[D_QK:D_V]], axis=0)],
        axis=1)
    rhs = jnp.concatenate([feat, qterm_ref[...]], axis=0)
    k_terms = kterm_ref[...]

    ones = jnp.ones((BF16_ROWS, tq), BF16)

    m_sc[...] = jnp.full(m_sc.shape, -jnp.inf, F32)
    acc_sc[...] = jnp.zeros(acc_sc.shape, F32)

    n_chunks = 2 * tq // ATTN_CHUNK
    chunk = lambda ci: slice(ci * ATTN_CHUNK, (ci + 1) * ATTN_CHUNK)

    def scores(kj, ci):
        start = pl.multiple_of(kj * tq, tq)
        lhs = jnp.concatenate([k_ref[pl.ds(start, tq), :], k_terms], axis=1)
        return _dot(lhs, rhs[:, chunk(ci)])

    def block(kj, rd, masked, last):
        vt = jnp.concatenate([vt_ref[kj], ones], axis=0)
        off = slope * ((qi - kj) * tq).astype(F32)
        for ci in range(n_chunks):
            cs = chunk(ci)
            s_ci = s_sc[rd, ci]
            if masked:
                key = lax.broadcasted_iota(jnp.int32, s_ci.shape, 0)
                qry = (lax.broadcasted_iota(jnp.int32, s_ci.shape, 1) + ci * ATTN_CHUNK) & (tq - 1)
                s_ci = jnp.where(qry >= key, s_ci, -jnp.inf)
            m_prev = m_sc[:, cs]
            m_new = jnp.maximum(m_prev, jnp.max(s_ci, axis=0, keepdims=True) - off)
            p = jnp.exp(s_ci - (m_new + off)).astype(BF16)
            alpha = jnp.exp(m_prev - m_new)
            if not last:
                s_sc[1 - rd, ci] = scores(kj + 1, ci)
            acc_sc[:, cs] = alpha * acc_sc[:, cs] + _dot(vt, p)
            m_sc[:, cs] = m_new

    for ci in range(n_chunks):
        s_sc[0, ci] = scores(0, ci)

    def pair(i, carry):
        block(2 * i, 0, masked=False, last=False)
        block(2 * i + 1, 1, masked=False, last=False)
        return carry

    lax.fori_loop(0, lax.shift_right_logical(qi, 1), pair, 0)

    @pl.when((qi & 1) == 1)
    def _():
        block(qi - 1, 0, masked=False, last=False)
        block(qi, 1, masked=True, last=True)

    @pl.when((qi & 1) == 0)
    def _():
        block(qi, 0, masked=True, last=True)

    lam = _diff_lambda(lam_ref, lam_init)
    acc = acc_sc[...]
    o1 = acc[0:D_V, 0:tq] / acc[D_V:D_V + 1, 0:tq]
    o2 = acc[0:D_V, tq:2 * tq] / acc[D_V:D_V + 1, tq:2 * tq]
    o = o1 - lam * o2
    o = o * lax.rsqrt(jnp.mean(o * o, axis=0, keepdims=True) + NORM_EPS)
    o_ref[...] = (o.T * g_ref[...] * (1.0 - lam_init)).astype(o_ref.dtype)


def _attn_call(slopes, qt, k_bf, vt, lamv, g_subln, lam_init, name):
    b, nt = qt.shape[0], qt.shape[1]
    s = nt * TQ
    const = lambda i, h, j: (0, 0)
    return pl.pallas_call(
        functools.partial(_attn_kernel, lam_init=lam_init),
        grid=(b, N_HEADS, nt),
        in_specs=[
            pl.BlockSpec(memory_space=pltpu.SMEM),
            pl.BlockSpec((None, None, D_V, TQ), lambda i, h, j: (i, j, h, 0)),
            pl.BlockSpec((None, None, s, D_V), lambda i, h, j: (i, h, 0, 0)),
            pl.BlockSpec((None, nt, D_V, TQ), lambda i, h, j: (i, 0, h, 0)),
            pl.BlockSpec((None, TQ, LANES), lambda i, h, j: (h, 0, 0)),
            pl.BlockSpec((None, LANES, 2 * TQ), lambda i, h, j: (h, 0, 0)),
            pl.BlockSpec((4, D_QK), const),
            pl.BlockSpec((1, D_V), const),
        ],
        out_specs=pl.BlockSpec((None, TQ, D_V), lambda i, h, j: (i, j, h)),
        out_shape=jax.ShapeDtypeStruct((b, s, D_ATT), BF16),
        scratch_shapes=[
            pltpu.VMEM((1, 2 * TQ), F32),
            pltpu.VMEM((D_V + BF16_ROWS, 2 * TQ), F32),
            pltpu.VMEM((2, 2 * TQ // ATTN_CHUNK, TQ, ATTN_CHUNK), F32),
        ],
        compiler_params=_cparams(3),
        name=name,
    )(slopes, qt, k_bf, vt, *_alibi_terms(TQ), lamv, g_subln)


def _dot_tn(a, b):
    return lax.dot_general(a, b, (((0,), (0,)), ((), ())), preferred_element_type=F32)


def _decode_one(q_ref, kn_ref, vn_ref, bias_ref, lam_ref, g_ref, k_pages, v_pages, o_ref, lam_init):
    page = k_pages[0].shape[0] // N_HEADS
    cols = BF16_ROWS

    def flat(ref):
        return jnp.concatenate([ref[h:h + 1, :] for h in range(N_HEADS)], axis=1)

    def dense(ref):
        return jnp.concatenate(
            [ref[pl.ds(h, page, stride=N_HEADS), :] for h in range(N_HEADS)], axis=1).astype(BF16)

    def first_row(x):
        r = lax.broadcasted_iota(jnp.int32, (BF16_ROWS, D_ATT), 0)
        return jnp.where(r == 0, jnp.broadcast_to(x, (BF16_ROWS, D_ATT)), 0.0).astype(BF16)

    r = lax.broadcasted_iota(jnp.int32, (LANES, D_ATT), 0)
    c = lax.broadcasted_iota(jnp.int32, (LANES, D_ATT), 1)
    col = ((c >> 6) & 1) * N_HEADS + (c >> 7)
    qt = jnp.where(r == col, jnp.broadcast_to(flat(q_ref), (LANES, D_ATT)), 0.0).astype(BF16)

    s = jnp.concatenate([_dot_nt(dense(kp), qt) for kp in k_pages], axis=0)
    s = s + bias_ref[...]
    s_new = _dot_nt(first_row(flat(kn_ref)), qt)
    s_new = jnp.where(lax.broadcasted_iota(jnp.int32, s_new.shape, 0) == 0, s_new, -jnp.inf)

    m = jnp.maximum(jnp.max(s, axis=0, keepdims=True), jnp.max(s_new, axis=0, keepdims=True))
    e = jnp.exp(s - m)
    e_new = jnp.exp(s_new - m)
    l = jnp.sum(e, axis=0, keepdims=True) + jnp.sum(e_new, axis=0, keepdims=True)
    p = (e / l)[:, 0:cols].astype(BF16)
    p_new = (e_new / l)[:, 0:cols].astype(BF16)

    acc = _dot_tn(p_new, first_row(flat(vn_ref)))
    for i, vp in enumerate(v_pages):
        acc = acc + _dot_tn(p[i * page:(i + 1) * page], dense(vp))

    lam = _diff_lambda(lam_ref, lam_init)
    for h in range(N_HEADS):
        o = (acc[h:h + 1, h * D_V:(h + 1) * D_V]
             - lam * acc[N_HEADS + h:N_HEADS + h + 1, h * D_V:(h + 1) * D_V])
        o_ref[:, h * D_V:(h + 1) * D_V] = (_rms(o, g_ref[...]) * (1.0 - lam_init)).astype(o_ref.dtype)


def _decode_attn_kernel(pt_ref, q_ref, kn_ref, vn_ref, bias_ref, lam_ref, g_ref, ck_hbm, cv_hbm,
                        o_ref, kbuf, vbuf, sem, *, layer, n_pages, lam_init):
    i = pl.program_id(0)
    n = pl.num_programs(0)

    def page_copies(seq, slot):
        out = []
        for j in range(n_pages):
            pg = pt_ref[seq, j]
            out.append(pltpu.make_async_copy(ck_hbm.at[layer, pg], kbuf.at[slot, j], sem.at[0, slot]))
            out.append(pltpu.make_async_copy(cv_hbm.at[layer, pg], vbuf.at[slot, j], sem.at[1, slot]))
        return out

    def start(seq, slot):
        for cp in page_copies(seq, slot):
            cp.start()

    def wait(seq, slot):
        for cp in page_copies(seq, slot):
            cp.wait()

    def compute(slot):
        _decode_one(q_ref.at[slot], kn_ref.at[slot], vn_ref.at[slot], bias_ref, lam_ref, g_ref,
                    [kbuf.at[slot, j] for j in range(n_pages)],
                    [vbuf.at[slot, j] for j in range(n_pages)], o_ref.at[slot], lam_init)

    per_step = kbuf.shape[0]

    @pl.when(i == 0)
    def _():
        for s in range(DECODE_AHEAD):
            start(s, s)

    for k in range(per_step):
        nxt = k + DECODE_AHEAD
        if nxt < per_step:
            start(per_step * i + nxt, nxt)
        else:
            @pl.when(i + 1 < n)
            def _(nxt=nxt):
                start(per_step * i + nxt, nxt - per_step)
        wait(per_step * i + k, k)
        compute(k)


def _decode_attn_call(page_table, q4, k4, v4, bias, lamv, g_subln, cache_k, cache_v,
                      layer, lam_init, name):
    b = q4.shape[0]
    n_pages = page_table.shape[1]
    slab = cache_k.shape[2]
    per_step = DECODE_SLOTS
    tok = lambda i, pt: (i, 0, 0)
    const = lambda i, pt: (0, 0)
    grid_spec = pltpu.PrefetchScalarGridSpec(
        num_scalar_prefetch=1,
        grid=(b // per_step,),
        in_specs=[
            pl.BlockSpec((per_step, N_HEADS, D_V), tok),
            pl.BlockSpec((per_step, N_HEADS, D_V), tok),
            pl.BlockSpec((per_step, N_HEADS, D_V), tok),
            pl.BlockSpec(bias.shape, const),
            pl.BlockSpec((4, D_QK), const),
            pl.BlockSpec((1, D_V), const),
            pl.BlockSpec(memory_space=pl.ANY),
            pl.BlockSpec(memory_space=pl.ANY),
        ],
        out_specs=pl.BlockSpec((per_step, 1, D_ATT), tok),
        scratch_shapes=[
            pltpu.VMEM((per_step, n_pages, slab, D_V), F32),
            pltpu.VMEM((per_step, n_pages, slab, D_V), F32),
            pltpu.SemaphoreType.DMA((2, per_step)),
        ],
    )
    return pl.pallas_call(
        functools.partial(_decode_attn_kernel, layer=layer, n_pages=n_pages, lam_init=lam_init),
        grid_spec=grid_spec,
        out_shape=jax.ShapeDtypeStruct((b, 1, D_ATT), BF16),
        compiler_params=_cparams(1),
        name=name,
    )(page_table, q4, k4, v4, bias, lamv, g_subln, cache_k, cache_v)


def _lru_gates(u, wg_ref, bg_ref, lam_ref):
    gates = jax.nn.sigmoid(_dot(u.astype(BF16), wg_ref[...]) + bg_ref[...])
    r = gates[:, :D_LRU]
    i_g = gates[:, D_LRU:]
    neg_lam = -lam_ref[...]
    softplus = jnp.maximum(neg_lam, 0.0) + jnp.log1p(jnp.exp(-jnp.abs(neg_lam)))
    log_a = -LRU_C * r * softplus
    a = jnp.exp(log_a)
    one_minus_a2 = -jnp.tanh(log_a) * (a * a + 1.0)
    root = jnp.where(one_minus_a2 > 0.0, one_minus_a2 * lax.rsqrt(one_minus_a2), 0.0)
    return a, root * (i_g * u)


def _lru_seq_kernel(xr_ref, gr_ref, cw_ref, cb_ref, wg_ref, bg_ref, lam_ref,
                    y_ref, h_ref, tail_ref, xbuf, hbuf):
    t = T_LRU
    j = pl.program_id(1)

    @pl.when(j == 0)
    def _():
        xbuf[0:SUBLANES, :] = jnp.zeros((SUBLANES, D_LRU), F32)
        hbuf[...] = jnp.zeros(hbuf.shape, F32)

    x = xr_ref[...]
    xbuf[SUBLANES:SUBLANES + t, :] = x
    cw = cw_ref[...]
    u = cb_ref[...] + cw[CONV_WIDTH - 1:CONV_WIDTH] * x
    for d in range(1, CONV_WIDTH):
        u = u + cw[CONV_WIDTH - 1 - d:CONV_WIDTH - d] * xbuf[SUBLANES - d:SUBLANES - d + t, :]

    a, b = _lru_gates(u, wg_ref, bg_ref, lam_ref)

    groups = t // SUBLANES
    a = a.reshape(groups, SUBLANES, D_LRU)
    b = b.reshape(groups, SUBLANES, D_LRU)
    row = lax.broadcasted_iota(jnp.int32, a.shape, 1)
    shift = 1
    while shift < SUBLANES:
        a_prev = pltpu.roll(a, shift, 1)
        b_prev = pltpu.roll(b, shift, 1)
        live = row >= shift
        b = jnp.where(live, a * b_prev + b, b)
        a = jnp.where(live, a * a_prev, a)
        shift *= 2

    h = hbuf[SUBLANES - 1:SUBLANES, :]
    rows = []
    for g in range(groups):
        rows.append(b[g] + a[g] * h)
        h = rows[-1][SUBLANES - 1:SUBLANES, :]
    hs = jnp.concatenate(rows, axis=0)
    y_ref[...] = (hs * jax.nn.gelu(gr_ref[...])).astype(y_ref.dtype)

    hbuf[...] = hs[t - SUBLANES:t]
    xbuf[0:SUBLANES, :] = x[t - SUBLANES:t]
    h_ref[...] = hs[t - SUBLANES:t]
    tail_ref[...] = x[t - SUBLANES:t]


def _lru_seq_call(xr, gr, conv_w, conv_b, wg_bf, bg, lam, name):
    b, s, _ = xr.shape
    tok = lambda i, j: (i, j, 0)
    const = lambda i, j: (0, 0)
    per_seq = lambda i, j: (i, 0, 0)
    return pl.pallas_call(
        _lru_seq_kernel,
        grid=(b, s // T_LRU),
        in_specs=[
            pl.BlockSpec((None, T_LRU, D_LRU), tok),
            pl.BlockSpec((None, T_LRU, D_LRU), tok),
            pl.BlockSpec((CONV_WIDTH, D_LRU), const),
            pl.BlockSpec((1, D_LRU), const),
            pl.BlockSpec((D_LRU, 2 * D_LRU), const),
            pl.BlockSpec((1, 2 * D_LRU), const),
            pl.BlockSpec((1, D_LRU), const),
        ],
        out_specs=[
            pl.BlockSpec((None, T_LRU, D_LRU), tok),
            pl.BlockSpec((None, SUBLANES, D_LRU), per_seq),
            pl.BlockSpec((None, SUBLANES, D_LRU), per_seq),
        ],
        out_shape=[
            jax.ShapeDtypeStruct((b, s, D_LRU), BF16),
            jax.ShapeDtypeStruct((b, SUBLANES, D_LRU), F32),
            jax.ShapeDtypeStruct((b, SUBLANES, D_LRU), F32),
        ],
        scratch_shapes=[
            pltpu.VMEM((SUBLANES + T_LRU, D_LRU), F32),
            pltpu.VMEM((SUBLANES, D_LRU), F32),
        ],
        compiler_params=_cparams(2),
        name=name,
    )(xr, gr, conv_w, conv_b, wg_bf, bg, lam)


def _lru_step_kernel(xr_ref, gr_ref, cbuf_ref, h0_ref, cw_ref, cb_ref, wg_ref, bg_ref, lam_ref,
                     y_ref, h_ref):
    x = xr_ref[...]
    cw = cw_ref[...]
    u = cb_ref[...] + cw[CONV_WIDTH - 1:CONV_WIDTH] * x
    for d in range(CONV_WIDTH - 1):
        u = u + cw[d:d + 1] * cbuf_ref[d]
    a, b = _lru_gates(u, wg_ref, bg_ref, lam_ref)
    h = a * h0_ref[...] + b
    h_ref[...] = h
    y_ref[...] = (h * jax.nn.gelu(gr_ref[...])).astype(y_ref.dtype)


def _lru_step_call(xr, gr, cbuf_t, h0, conv_w, conv_b, wg_bf, bg, lam, name):
    b = xr.shape[0]
    return pl.pallas_call(
        _lru_step_kernel,
        out_shape=[jax.ShapeDtypeStruct((b, D_LRU), BF16),
                   jax.ShapeDtypeStruct((b, D_LRU), F32)],
        compiler_params=pltpu.CompilerParams(vmem_limit_bytes=VMEM_LIMIT_BYTES),
        name=name,
    )(xr, gr, cbuf_t, h0, conv_w, conv_b, wg_bf, bg, lam)


def _outffn_kernel(x_ref, oa_ref, yl_ref, gtm_ref, shf_ref, scf_ref, gtf_ref,
                   gpm_ref, gpf_ref, gqf_ref, wo_ref, wi_ref, wf_ref, o_ref):
    d_ff = wf_ref.shape[0]
    tm = x_ref.shape[0]
    n_groups = FFN_ROW_GROUPS if tm % (FFN_ROW_GROUPS * BF16_ROWS) == 0 else 1
    rows = [pl.ds(g * (tm // n_groups), tm // n_groups) for g in range(n_groups)]
    per_tok = lambda ref, r: ref[r, :] if ref.shape[0] == tm else ref[...]

    mix = [_dot(oa_ref[r, :], wo_ref[0:D_ATT, :]) + _dot(yl_ref[r, :], wo_ref[D_ATT:D_MODEL, :])
           for r in rows]
    x1 = [x_ref[r, :] + per_tok(gtm_ref, r) * _rms(m, gpm_ref[...]) for r, m in zip(rows, mix)]
    h = [(_rms(x, gpf_ref[...]) * (1.0 + per_tok(scf_ref, r)) + per_tok(shf_ref, r)).astype(BF16)
         for r, x in zip(rows, x1)]
    act = []
    for hg in h:
        gate = _dot(hg, wi_ref[:, 0:d_ff])
        up = _dot(hg, wi_ref[:, d_ff:2 * d_ff])
        act.append((gate * jax.nn.sigmoid(gate) * up).astype(BF16))
    f = [_dot(a, wf_ref[...]) for a in act]
    for r, x, fg in zip(rows, x1, f):
        o_ref[r, :] = x + per_tok(gtf_ref, r) * _rms(fg, gqf_ref[...])


def _outffn_call(x, oa, yl, gtm, shf, scf, gtf, gpm, gpf, gqf, wo_bf, wi_bf, wf_bf, tm, layer, name):
    b, s, _ = x.shape
    r = gtm.shape[1]
    rb = 1 if r == 1 else tm
    d_ff = wf_bf.shape[1]
    tok = lambda i, j: (i, j, 0)
    slab = lambda i, j: (layer, 0, 0)
    mod = (lambda i, j: (i, 0, 0)) if r == 1 else tok
    const = lambda i, j: (0, 0)
    resident = pl.Buffered(1)
    return pl.pallas_call(
        _outffn_kernel,
        grid=(b, s // tm),
        in_specs=[
            pl.BlockSpec((None, tm, D_MODEL), tok),
            pl.BlockSpec((None, tm, D_ATT), tok),
            pl.BlockSpec((None, tm, D_LRU), tok),
            pl.BlockSpec((None, rb, D_MODEL), mod),
            pl.BlockSpec((None, rb, D_MODEL), mod),
            pl.BlockSpec((None, rb, D_MODEL), mod),
            pl.BlockSpec((None, rb, D_MODEL), mod),
            pl.BlockSpec((1, D_MODEL), const),
            pl.BlockSpec((1, D_MODEL), const),
            pl.BlockSpec((1, D_MODEL), const),
            pl.BlockSpec((None, D_MODEL, D_MODEL), slab, pipeline_mode=resident),
            pl.BlockSpec((None, D_MODEL, 2 * d_ff), slab, pipeline_mode=resident),
            pl.BlockSpec((None, d_ff, D_MODEL), slab, pipeline_mode=resident),
        ],
        out_specs=pl.BlockSpec((None, tm, D_MODEL), tok),
        out_shape=jax.ShapeDtypeStruct((b, s, D_MODEL), F32),
        compiler_params=_cparams(2),
        name=name,
    )(x, oa, yl, gtm, shf, scf, gtf, gpm, gpf, gqf, wo_bf, wi_bf, wf_bf)


def _block_diag(w):
    n, i, j = w.shape
    eye = jnp.eye(n, dtype=w.dtype)
    return (eye[:, None, :, None] * w[:, :, None, :]).reshape(n * i, n * j)


def kernel(x_prompt, x_sample, cache_k, cache_v, state_h, state_conv, page_table, c_prompt, c_sample,
           w_ada, b_ada, g_pre_mix, w_in, lam_q1, lam_k1, lam_q2, lam_k2, g_subln, conv_w, conv_b,
           lru_wa, lru_ba, lru_wx, lru_bx, lru_lambda, w_out, g_post_mix, g_pre_ffn, w_ffn_in,
           w_ffn_out, g_post_ffn):
    depth = w_ada.shape[0]
    bp, seq, _ = x_prompt.shape
    bs = x_sample.shape[0]
    page = cache_k.shape[2]
    past = page_table.shape[1] * page

    pad = (-(bs + bp)) % SUBLANES
    c_all = jnp.concatenate([c_sample, c_prompt, jnp.zeros((pad, D_MODEL), F32)], axis=0)
    mod = _ada_call(c_all, w_ada, b_ada)

    ys = x_sample.reshape(1, bs, D_MODEL)
    yp = x_prompt

    slope_of = lambda h: 2.0 ** (-8.0 * (h + 1) / N_HEADS)
    slopes = jnp.asarray([slope_of(h) for h in range(N_HEADS)], F32)
    dec_bias = np.zeros((past, LANES), np.float32)
    for j in range(2 * N_HEADS):
        dec_bias[:, j] = -slope_of(j % N_HEADS) * (past - np.arange(past, dtype=np.float32))
    dec_bias = jnp.asarray(dec_bias)
    ck = cache_k.reshape(depth, cache_k.shape[1], page * N_HEADS, D_V)
    cv = cache_v.reshape(depth, cache_v.shape[1], page * N_HEADS, D_V)

    w_in_bf, wo_bf, wi_bf, wf_bf = (w.astype(BF16) for w in (w_in, w_out, w_ffn_in, w_ffn_out))

    outs = {n: [] for n in ("hp", "cp", "ks", "vs", "hs", "cs")}
    kv_stack = ()
    for l in range(depth):
        lam_init = 0.8 - 0.6 * math.exp(-0.3 * l)
        lamv = jnp.stack([lam_q1[l], lam_k1[l], lam_q2[l], lam_k2[l]])
        g_sub = g_subln[l].reshape(1, D_V)
        wg_bf = jnp.concatenate([_block_diag(lru_wa[l]), _block_diag(lru_wx[l])], axis=1).astype(BF16)
        bg = jnp.concatenate([lru_ba[l].reshape(1, D_LRU), lru_bx[l].reshape(1, D_LRU)], axis=1)
        lam_row = lru_lambda[l].reshape(1, D_LRU)
        cb_row = conv_b[l].reshape(1, D_LRU)
        row = lambda v: v[l].reshape(1, D_MODEL)

        mods = [mod[l, :, i * D_MODEL:(i + 1) * D_MODEL] for i in range(6)]
        mod_s = [m[:bs].reshape(1, bs, D_MODEL) for m in mods]
        mod_p = [m[bs:bs + bp].reshape(bp, 1, D_MODEL) for m in mods]

        qt, kp_all, kb, vp_all, vt, xr, gr = _inproj_call(
            yp, mod_p[0], mod_p[1], row(g_pre_mix), w_in_bf, TM_PROJ, True, f"inproj_p{l}",
            layer=l, depth=depth, kv_stack=kv_stack)
        kv_stack = (kp_all, vp_all)
        oa = _attn_call(slopes, qt, kb, vt, lamv, g_sub, lam_init, f"attn_p{l}")
        yl, h8, x8 = _lru_seq_call(xr, gr, conv_w[l], cb_row, wg_bf, bg, lam_row, f"lru_p{l}")
        yp = _outffn_call(yp, oa, yl, mod_p[2], mod_p[3], mod_p[4], mod_p[5],
                          row(g_post_mix), row(g_pre_ffn), row(g_post_ffn),
                          wo_bf, wi_bf, wf_bf, TM_FFN, l, f"outffn_p{l}")
        outs["hp"].append(h8[:, SUBLANES - 1])
        outs["cp"].append(x8[:, SUBLANES - (CONV_WIDTH - 1):])

        q4, k4, v4, xr, gr = _inproj_call(ys, mod_s[0], mod_s[1], row(g_pre_mix), w_in_bf,
                                          bs, False, f"inproj_s{l}", layer=l)
        per_token = lambda a: a.reshape(bs, N_HEADS, D_V)
        oa = _decode_attn_call(page_table, per_token(q4), per_token(k4), per_token(v4), dec_bias, lamv, g_sub,
                               ck, cv, l, lam_init, f"attn_s{l}")
        yl, h_new = _lru_step_call(xr[0], gr[0], jnp.swapaxes(state_conv[l], 0, 1), state_h[l],
                                   conv_w[l], cb_row, wg_bf, bg, lam_row, f"lru_s{l}")
        ys = _outffn_call(ys, oa.reshape(1, bs, D_ATT), yl.reshape(1, bs, D_LRU),
                          mod_s[2], mod_s[3], mod_s[4], mod_s[5],
                          row(g_post_mix), row(g_pre_ffn), row(g_post_ffn),
                          wo_bf, wi_bf, wf_bf, bs, l, f"outffn_s{l}")
        outs["ks"].append(k4.reshape(bs, 1, N_HEADS, D_V))
        outs["vs"].append(v4.reshape(bs, 1, N_HEADS, D_V))
        outs["hs"].append(h_new)
        outs["cs"].append(jnp.concatenate([state_conv[l][:, 1:], xr[0][:, None, :]], axis=1))

    st = {n: jnp.stack(v) for n, v in outs.items()}
    kp_all, vp_all = (a.reshape(depth, bp, seq, N_HEADS, D_V) for a in kv_stack)
    return (yp, ys.reshape(bs, 1, D_MODEL), kp_all, vp_all, st["hp"], st["cp"],
            st["ks"], st["vs"], st["hs"], st["cs"])
```

```python
import functools
import math

import numpy as np
import jax
import jax.numpy as jnp
from jax import lax
from jax.experimental import pallas as pl
from jax.experimental.pallas import tpu as pltpu

D_MODEL = 1024
N_HEADS = 4
D_QK = 64
D_V = 2 * D_QK
D_ATT = N_HEADS * D_V
D_LRU = D_MODEL - D_ATT
N_LRU_BLOCKS = 8
CONV_WIDTH = 4
LRU_C = 8.0
D_IN = 3 * D_ATT + 2 * D_LRU
NORM_EPS = 1e-6
QK_SCALE = D_QK ** -0.5

F32 = jnp.float32
BF16 = jnp.bfloat16

VMEM_LIMIT_BYTES = 56 * 1024 * 1024

SUBLANES = 8
BF16_ROWS = 16
LANES = 128

TM_PROJ = 512
TM_FFN = 512
TQ = TM_PROJ
ATTN_CHUNK = 512
FFN_ROW_GROUPS = 2
T_LRU = 512
DECODE_SLOTS = 4
DECODE_AHEAD = 2
ADA_TN = 1536
BF16_EXACT_INT = 256


def _cparams(n_axes):
    return pltpu.CompilerParams(
        dimension_semantics=("arbitrary",) * n_axes,
        vmem_limit_bytes=VMEM_LIMIT_BYTES)


def _rms(x, g):
    return x * lax.rsqrt(jnp.mean(x * x, axis=-1, keepdims=True) + NORM_EPS) * g


def _dot(a, b):
    return jnp.dot(a, b, preferred_element_type=F32)


def _dot_nt(a, b):
    return lax.dot_general(a, b, (((1,), (1,)), ((), ())), preferred_element_type=F32)


def _ada_kernel(c_ref, w_ref, b_ref, o_ref):
    c = c_ref[...]
    h = (c * jax.nn.sigmoid(c)).astype(BF16)
    o_ref[...] = _dot(h, w_ref[...].astype(BF16)) + b_ref[...]


def _ada_call(c_all, w_ada, b_ada):
    depth, _, n = w_ada.shape
    rows = c_all.shape[0]
    return pl.pallas_call(
        _ada_kernel,
        grid=(depth, n // ADA_TN),
        in_specs=[
            pl.BlockSpec((rows, D_MODEL), lambda l, j: (0, 0)),
            pl.BlockSpec((None, D_MODEL, ADA_TN), lambda l, j: (l, 0, j)),
            pl.BlockSpec((None, 1, ADA_TN), lambda l, j: (l, 0, j)),
        ],
        out_specs=pl.BlockSpec((None, rows, ADA_TN), lambda l, j: (l, 0, j)),
        out_shape=jax.ShapeDtypeStruct((depth, rows, n), F32),
        compiler_params=_cparams(2),
        name="ada_mod",
    )(c_all, w_ada, b_ada.reshape(depth, 1, n))


def _store_heads(ref, x):
    tm = x.shape[0]
    for h in range(N_HEADS):
        ref[pl.ds(h, tm, stride=N_HEADS), :] = x[:, h * D_V:(h + 1) * D_V]


def _inproj_kernel(x_ref, sh_ref, sc_ref, g_ref, w_ref, *rest, prompt, layer, depth):
    x = x_ref[...]
    h = (_rms(x, g_ref[...]) * (1.0 + sc_ref[...]) + sh_ref[...]).astype(BF16)
    q = _dot(h, w_ref[:, 0:D_ATT]) * QK_SCALE
    k = _dot(h, w_ref[:, D_ATT:2 * D_ATT])
    v = _dot(h, w_ref[:, 2 * D_ATT:3 * D_ATT])
    if prompt:
        qt_ref, k4_ref, kb_ref, v4_ref, vt_ref, xr_ref, gr_ref = rest[-7:]
        qt_ref[...] = q.T.astype(BF16)
        for hd in range(N_HEADS):
            kb_ref[hd] = k[:, hd * D_V:(hd + 1) * D_V].astype(BF16)
        vt_ref[...] = v.T.astype(BF16)
        if layer == 0:
            for ref in (k4_ref, v4_ref):
                for d in range(1, depth):
                    ref[d] = jnp.zeros(ref.shape[1:], ref.dtype)
            k4_ref, v4_ref = k4_ref.at[0], v4_ref.at[0]
    else:
        q4_ref, k4_ref, v4_ref, xr_ref, gr_ref = rest
        _store_heads(q4_ref, q)
    _store_heads(k4_ref, k)
    _store_heads(v4_ref, v)
    xr_ref[...] = _dot(h, w_ref[:, 3 * D_ATT:3 * D_ATT + D_LRU])
    gr_ref[...] = _dot(h, w_ref[:, 3 * D_ATT + D_LRU:D_IN])


def _inproj_call(x, sh, sc, g, w_bf, tm, prompt, name, layer=0, depth=1, kv_stack=()):
    b, s, _ = x.shape
    nt = s // tm
    r = sh.shape[1]
    rb = 1 if r == 1 else tm
    tok = lambda i, j: (i, j, 0)
    tok4 = lambda i, j: (i, j, 0, 0)
    mod = (lambda i, j: (i, 0, 0)) if r == 1 else tok
    heads = (jax.ShapeDtypeStruct((b, s * N_HEADS, D_V), F32),
             pl.BlockSpec((None, tm * N_HEADS, D_V), tok))
    stack_shape = jax.ShapeDtypeStruct((depth, b, s * N_HEADS, D_V), F32)
    if layer == 0:
        stack = (stack_shape, pl.BlockSpec((depth, None, tm * N_HEADS, D_V), lambda i, j: (0, i, j, 0)))
    else:
        stack = (stack_shape, pl.BlockSpec((None, None, tm * N_HEADS, D_V), lambda i, j: (layer, i, j, 0)))
    flat32 = (jax.ShapeDtypeStruct((b, s, D_LRU), F32), pl.BlockSpec((None, tm, D_LRU), tok))
    flat16 = (jax.ShapeDtypeStruct((b, N_HEADS, s, D_V), BF16),
              pl.BlockSpec((None, N_HEADS, tm, D_V), lambda i, j: (i, 0, j, 0)))
    tiles_t = (jax.ShapeDtypeStruct((b, nt, D_ATT, tm), BF16),
               pl.BlockSpec((None, None, D_ATT, tm), tok4))
    if prompt:
        outs = [tiles_t, stack, flat16, stack, tiles_t, flat32, flat32]
    else:
        outs = [heads, heads, heads, flat32, flat32]
    n_in = 5
    aliases = {n_in + n: out for n, out in zip(range(len(kv_stack)), (1, 3))}
    return pl.pallas_call(
        functools.partial(_inproj_kernel, prompt=prompt, layer=layer, depth=depth),
        grid=(b, nt),
        in_specs=[
            pl.BlockSpec((None, tm, D_MODEL), tok),
            pl.BlockSpec((None, rb, D_MODEL), mod),
            pl.BlockSpec((None, rb, D_MODEL), mod),
            pl.BlockSpec((1, D_MODEL), lambda i, j: (0, 0)),
            pl.BlockSpec((None, D_MODEL, D_IN), lambda i, j: (layer, 0, 0)),
        ] + [pl.BlockSpec(memory_space=pl.ANY) for _ in kv_stack],
        out_specs=[o[1] for o in outs],
        out_shape=[o[0] for o in outs],
        input_output_aliases=aliases,
        compiler_params=_cparams(2),
        name=name,
    )(x, sh, sc, g, w_bf, *kv_stack)


def _diff_lambda(lam_ref, lam_init):
    lv = lam_ref[...]
    s1 = jnp.sum(lv[0:1] * lv[1:2], axis=-1, keepdims=True)
    s2 = jnp.sum(lv[2:3] * lv[3:4], axis=-1, keepdims=True)
    return jnp.exp(s1) - jnp.exp(s2) + lam_init


def _alibi_terms(tq):
    off = np.arange(tq)
    lo = (off % BF16_EXACT_INT).astype(np.float32)
    hi = (off - off % BF16_EXACT_INT).astype(np.float32)
    kterm = np.zeros((N_HEADS, tq, LANES), np.float32)
    qterm = np.zeros((N_HEADS, LANES, 2 * tq), np.float32)
    for h in range(N_HEADS):
        slope = 2.0 ** (-8.0 * (h + 1) / N_HEADS)
        kterm[h, :, 0] = slope * lo
        kterm[h, :, 1] = slope * hi
        kterm[h, :, 2:4] = 1.0
        qterm[h, 0:2, :] = 1.0
        qterm[h, 2, :] = np.tile(-slope * lo, 2)
        qterm[h, 3, :] = np.tile(-slope * hi, 2)
    return jnp.asarray(kterm, BF16), jnp.asarray(qterm, BF16)


def _attn_kernel(slope_ref, qt_ref, k_ref, vt_ref, kterm_ref, qterm_ref, lam_ref, g_ref, o_ref,
                 m_sc, acc_sc, s_sc, smax_sc, *, lam_init):
    h = pl.program_id(1)
    qi = pl.program_id(2)
    slope = slope_ref[h]
    tq = TQ

    qt = qt_ref[...]
    zq = jnp.zeros((D_QK, tq), BF16)
    feat = jnp.concatenate(
        [jnp.concatenate([qt[0:D_QK], zq], axis=0), jnp.concatenate([zq, qt[D_QK:D_V]], axis=0)],
        axis=1)
    rhs = jnp.concatenate([feat, qterm_ref[...]], axis=0)
    k_terms = kterm_ref[...]

    ones = jnp.ones((BF16_ROWS, tq), BF16)

    m_sc[...] = jnp.full(m_sc.shape, -jnp.inf, F32)
    acc_sc[...] = jnp.zeros(acc_sc.shape, F32)

    n_chunks = 2 * tq // ATTN_CHUNK
    chunk = lambda ci: slice(ci * ATTN_CHUNK, (ci + 1) * ATTN_CHUNK)

    def scores(kj, ci):
        start = pl.multiple_of(kj * tq, tq)
        lhs = jnp.concatenate([k_ref[pl.ds(start, tq), :], k_terms], axis=1)
        return _dot(lhs, rhs[:, chunk(ci)])

    def put_scores(buf, ci, kj):
        s = scores(kj, ci)
        s_sc[buf, ci] = s
        smax_sc[buf, ci] = jnp.max(s, axis=0, keepdims=True)

    def block(kj, rd, masked, last):
        vt = jnp.concatenate([vt_ref[kj], ones], axis=0)
        off = slope * ((qi - kj) * tq).astype(F32)
        for ci in range(n_chunks):
            cs = chunk(ci)
            s_ci = s_sc[rd, ci]
            if masked:
                key = lax.broadcasted_iota(jnp.int32, s_ci.shape, 0)
                qry = (lax.broadcasted_iota(jnp.int32, s_ci.shape, 1) + ci * ATTN_CHUNK) & (tq - 1)
                s_ci = jnp.where(qry >= key, s_ci, -jnp.inf)
            s_max = jnp.max(s_ci, axis=0, keepdims=True) if masked else smax_sc[rd, ci]
            m_prev = m_sc[:, cs]
            m_new = jnp.maximum(m_prev, s_max - off)
            p = jnp.exp(s_ci - (m_new + off)).astype(BF16)
            alpha = jnp.exp(m_prev - m_new)
            if not last:
                put_scores(1 - rd, ci, kj + 1)
            acc_sc[:, cs] = alpha * acc_sc[:, cs] + _dot(vt, p)
            m_sc[:, cs] = m_new

    for ci in range(n_chunks):
        put_scores(0, ci, 0)

    def pair(i, carry):
        block(2 * i, 0, masked=False, last=False)
        block(2 * i + 1, 1, masked=False, last=False)
        return carry

    lax.fori_loop(0, lax.shift_right_logical(qi, 1), pair, 0)

    @pl.when((qi & 1) == 1)
    def _():
        block(qi - 1, 0, masked=False, last=False)
        block(qi, 1, masked=True, last=True)

    @pl.when((qi & 1) == 0)
    def _():
        block(qi, 0, masked=True, last=True)

    lam = _diff_lambda(lam_ref, lam_init)
    acc = acc_sc[...]
    o1 = acc[0:D_V, 0:tq] / acc[D_V:D_V + 1, 0:tq]
    o2 = acc[0:D_V, tq:2 * tq] / acc[D_V:D_V + 1, tq:2 * tq]
    o = o1 - lam * o2
    o = o * lax.rsqrt(jnp.mean(o * o, axis=0, keepdims=True) + NORM_EPS)
    o_ref[...] = (o.T * g_ref[...] * (1.0 - lam_init)).astype(o_ref.dtype)


def _attn_call(slopes, qt, k_bf, vt, lamv, g_subln, lam_init, name):
    b, nt = qt.shape[0], qt.shape[1]
    s = nt * TQ
    const = lambda i, h, j: (0, 0)
    return pl.pallas_call(
        functools.partial(_attn_kernel, lam_init=lam_init),
        grid=(b, N_HEADS, nt),
        in_specs=[
            pl.BlockSpec(memory_space=pltpu.SMEM),
            pl.BlockSpec((None, None, D_V, TQ), lambda i, h, j: (i, j, h, 0)),
            pl.BlockSpec((None, None, s, D_V), lambda i, h, j: (i, h, 0, 0)),
            pl.BlockSpec((None, nt, D_V, TQ), lambda i, h, j: (i, 0, h, 0)),
            pl.BlockSpec((None, TQ, LANES), lambda i, h, j: (h, 0, 0)),
            pl.BlockSpec((None, LANES, 2 * TQ), lambda i, h, j: (h, 0, 0)),
            pl.BlockSpec((4, D_QK), const),
            pl.BlockSpec((1, D_V), const),
        ],
        out_specs=pl.BlockSpec((None, TQ, D_V), lambda i, h, j: (i, j, h)),
        out_shape=jax.ShapeDtypeStruct((b, s, D_ATT), BF16),
        scratch_shapes=[
            pltpu.VMEM((1, 2 * TQ), F32),
            pltpu.VMEM((D_V + BF16_ROWS, 2 * TQ), F32),
            pltpu.VMEM((2, 2 * TQ // ATTN_CHUNK, TQ, ATTN_CHUNK), F32),
            pltpu.VMEM((2, 2 * TQ // ATTN_CHUNK, 1, ATTN_CHUNK), F32),
        ],
        compiler_params=_cparams(3),
        name=name,
    )(slopes, qt, k_bf, vt, *_alibi_terms(TQ), lamv, g_subln)


def _dot_tn(a, b):
    return lax.dot_general(a, b, (((0,), (0,)), ((), ())), preferred_element_type=F32)


def _decode_one(q_ref, kn_ref, vn_ref, bias_ref, lam_ref, g_ref, k_pages, v_pages, o_ref, lam_init):
    page = k_pages[0].shape[0] // N_HEADS
    cols = BF16_ROWS

    def flat(ref):
        return jnp.concatenate([ref[h:h + 1, :] for h in range(N_HEADS)], axis=1)

    def dense(ref):
        return jnp.concatenate(
            [ref[pl.ds(h, page, stride=N_HEADS), :] for h in range(N_HEADS)], axis=1).astype(BF16)

    def first_row(x):
        r = lax.broadcasted_iota(jnp.int32, (BF16_ROWS, D_ATT), 0)
        return jnp.where(r == 0, jnp.broadcast_to(x, (BF16_ROWS, D_ATT)), 0.0).astype(BF16)

    r = lax.broadcasted_iota(jnp.int32, (LANES, D_ATT), 0)
    c = lax.broadcasted_iota(jnp.int32, (LANES, D_ATT), 1)
    col = ((c >> 6) & 1) * N_HEADS + (c >> 7)
    qt = jnp.where(r == col, jnp.broadcast_to(flat(q_ref), (LANES, D_ATT)), 0.0).astype(BF16)

    s = jnp.concatenate([_dot_nt(dense(kp), qt) for kp in k_pages], axis=0)
    s = s + bias_ref[...]
    s_new = _dot_nt(first_row(flat(kn_ref)), qt)
    s_new = jnp.where(lax.broadcasted_iota(jnp.int32, s_new.shape, 0) == 0, s_new, -jnp.inf)

    m = jnp.maximum(jnp.max(s, axis=0, keepdims=True), jnp.max(s_new, axis=0, keepdims=True))
    e = jnp.exp(s - m)
    e_new = jnp.exp(s_new - m)
    l = jnp.sum(e, axis=0, keepdims=True) + jnp.sum(e_new, axis=0, keepdims=True)
    p = (e / l)[:, 0:cols].astype(BF16)
    p_new = (e_new / l)[:, 0:cols].astype(BF16)

    acc = _dot_tn(p_new, first_row(flat(vn_ref)))
    for i, vp in enumerate(v_pages):
        acc = acc + _dot_tn(p[i * page:(i + 1) * page], dense(vp))

    lam = _diff_lambda(lam_ref, lam_init)
    for h in range(N_HEADS):
        o = (acc[h:h + 1, h * D_V:(h + 1) * D_V]
             - lam * acc[N_HEADS + h:N_HEADS + h + 1, h * D_V:(h + 1) * D_V])
        o_ref[:, h * D_V:(h + 1) * D_V] = (_rms(o, g_ref[...]) * (1.0 - lam_init)).astype(o_ref.dtype)


def _decode_attn_kernel(pt_ref, q_ref, kn_ref, vn_ref, bias_ref, lam_ref, g_ref, ck_hbm, cv_hbm,
                        o_ref, kbuf, vbuf, sem, *, layer, n_pages, lam_init):
    i = pl.program_id(0)
    n = pl.num_programs(0)

    def page_copies(seq, slot):
        out = []
        for j in range(n_pages):
            pg = pt_ref[seq, j]
            out.append(pltpu.make_async_copy(ck_hbm.at[layer, pg], kbuf.at[slot, j], sem.at[0, slot]))
            out.append(pltpu.make_async_copy(cv_hbm.at[layer, pg], vbuf.at[slot, j], sem.at[1, slot]))
        return out

    def start(seq, slot):
        for cp in page_copies(seq, slot):
            cp.start()

    def wait(seq, slot):
        for cp in page_copies(seq, slot):
            cp.wait()

    def compute(slot):
        _decode_one(q_ref.at[slot], kn_ref.at[slot], vn_ref.at[slot], bias_ref, lam_ref, g_ref,
                    [kbuf.at[slot, j] for j in range(n_pages)],
                    [vbuf.at[slot, j] for j in range(n_pages)], o_ref.at[slot], lam_init)

    per_step = kbuf.shape[0]

    @pl.when(i == 0)
    def _():
        for s in range(DECODE_AHEAD):
            start(s, s)

    for k in range(per_step):
        nxt = k + DECODE_AHEAD
        if nxt < per_step:
            start(per_step * i + nxt, nxt)
        else:
            @pl.when(i + 1 < n)
            def _(nxt=nxt):
                start(per_step * i + nxt, nxt - per_step)
        wait(per_step * i + k, k)
        compute(k)


def _decode_attn_call(page_table, q4, k4, v4, bias, lamv, g_subln, cache_k, cache_v,
                      layer, lam_init, name):
    b = q4.shape[0]
    n_pages = page_table.shape[1]
    slab = cache_k.shape[2]
    per_step = DECODE_SLOTS
    tok = lambda i, pt: (i, 0, 0)
    const = lambda i, pt: (0, 0)
    grid_spec = pltpu.PrefetchScalarGridSpec(
        num_scalar_prefetch=1,
        grid=(b // per_step,),
        in_specs=[
            pl.BlockSpec((per_step, N_HEADS, D_V), tok),
            pl.BlockSpec((per_step, N_HEADS, D_V), tok),
            pl.BlockSpec((per_step, N_HEADS, D_V), tok),
            pl.BlockSpec(bias.shape, const),
            pl.BlockSpec((4, D_QK), const),
            pl.BlockSpec((1, D_V), const),
            pl.BlockSpec(memory_space=pl.ANY),
            pl.BlockSpec(memory_space=pl.ANY),
        ],
        out_specs=pl.BlockSpec((per_step, 1, D_ATT), tok),
        scratch_shapes=[
            pltpu.VMEM((per_step, n_pages, slab, D_V), F32),
            pltpu.VMEM((per_step, n_pages, slab, D_V), F32),
            pltpu.SemaphoreType.DMA((2, per_step)),
        ],
    )
    return pl.pallas_call(
        functools.partial(_decode_attn_kernel, layer=layer, n_pages=n_pages, lam_init=lam_init),
        grid_spec=grid_spec,
        out_shape=jax.ShapeDtypeStruct((b, 1, D_ATT), BF16),
        compiler_params=_cparams(1),
        name=name,
    )(page_table, q4, k4, v4, bias, lamv, g_subln, cache_k, cache_v)


def _lru_gates(u, wg_ref, bg_ref, lam_ref):
    gates = jax.nn.sigmoid(_dot(u.astype(BF16), wg_ref[...]) + bg_ref[...])
    r = gates[:, :D_LRU]
    i_g = gates[:, D_LRU:]
    neg_lam = -lam_ref[...]
    softplus = jnp.maximum(neg_lam, 0.0) + jnp.log1p(jnp.exp(-jnp.abs(neg_lam)))
    log_a = -LRU_C * r * softplus
    a = jnp.exp(log_a)
    one_minus_a2 = -jnp.tanh(log_a) * (a * a + 1.0)
    root = jnp.where(one_minus_a2 > 0.0, one_minus_a2 * lax.rsqrt(one_minus_a2), 0.0)
    return a, root * (i_g * u)


def _lru_seq_kernel(xr_ref, gr_ref, cw_ref, cb_ref, wg_ref, bg_ref, lam_ref,
                    y_ref, h_ref, tail_ref, xbuf, hbuf):
    t = T_LRU
    j = pl.program_id(1)

    @pl.when(j == 0)
    def _():
        xbuf[0:SUBLANES, :] = jnp.zeros((SUBLANES, D_LRU), F32)
        hbuf[...] = jnp.zeros(hbuf.shape, F32)

    x = xr_ref[...]
    xbuf[SUBLANES:SUBLANES + t, :] = x
    cw = cw_ref[...]
    u = cb_ref[...] + cw[CONV_WIDTH - 1:CONV_WIDTH] * x
    for d in range(1, CONV_WIDTH):
        u = u + cw[CONV_WIDTH - 1 - d:CONV_WIDTH - d] * xbuf[SUBLANES - d:SUBLANES - d + t, :]

    a, b = _lru_gates(u, wg_ref, bg_ref, lam_ref)

    groups = t // SUBLANES
    a = a.reshape(groups, SUBLANES, D_LRU)
    b = b.reshape(groups, SUBLANES, D_LRU)
    row = lax.broadcasted_iota(jnp.int32, a.shape, 1)
    shift = 1
    while shift < SUBLANES:
        a_prev = pltpu.roll(a, shift, 1)
        b_prev = pltpu.roll(b, shift, 1)
        live = row >= shift
        b = jnp.where(live, a * b_prev + b, b)
        a = jnp.where(live, a * a_prev, a)
        shift *= 2

    h = hbuf[SUBLANES - 1:SUBLANES, :]
    rows = []
    for g in range(groups):
        rows.append(b[g] + a[g] * h)
        h = rows[-1][SUBLANES - 1:SUBLANES, :]
    hs = jnp.concatenate(rows, axis=0)
    y_ref[...] = (hs * jax.nn.gelu(gr_ref[...])).astype(y_ref.dtype)

    hbuf[...] = hs[t - SUBLANES:t]
    xbuf[0:SUBLANES, :] = x[t - SUBLANES:t]
    h_ref[...] = hs[t - SUBLANES:t]
    tail_ref[...] = x[t - SUBLANES:t]


def _lru_seq_call(xr, gr, conv_w, conv_b, wg_bf, bg, lam, name):
    b, s, _ = xr.shape
    tok = lambda i, j: (i, j, 0)
    const = lambda i, j: (0, 0)
    per_seq = lambda i, j: (i, 0, 0)
    return pl.pallas_call(
        _lru_seq_kernel,
        grid=(b, s // T_LRU),
        in_specs=[
            pl.BlockSpec((None, T_LRU, D_LRU), tok),
            pl.BlockSpec((None, T_LRU, D_LRU), tok),
            pl.BlockSpec((CONV_WIDTH, D_LRU), const),
            pl.BlockSpec((1, D_LRU), const),
            pl.BlockSpec((D_LRU, 2 * D_LRU), const),
            pl.BlockSpec((1, 2 * D_LRU), const),
            pl.BlockSpec((1, D_LRU), const),
        ],
        out_specs=[
            pl.BlockSpec((None, T_LRU, D_LRU), tok),
            pl.BlockSpec((None, SUBLANES, D_LRU), per_seq),
            pl.BlockSpec((None, SUBLANES, D_LRU), per_seq),
        ],
        out_shape=[
            jax.ShapeDtypeStruct((b, s, D_LRU), BF16),
            jax.ShapeDtypeStruct((b, SUBLANES, D_LRU), F32),
            jax.ShapeDtypeStruct((b, SUBLANES, D_LRU), F32),
        ],
        scratch_shapes=[
            pltpu.VMEM((SUBLANES + T_LRU, D_LRU), F32),
            pltpu.VMEM((SUBLANES, D_LRU), F32),
        ],
        compiler_params=_cparams(2),
        name=name,
    )(xr, gr, conv_w, conv_b, wg_bf, bg, lam)


def _lru_step_kernel(xr_ref, gr_ref, cbuf_ref, h0_ref, cw_ref, cb_ref, wg_ref, bg_ref, lam_ref,
                     y_ref, h_ref):
    x = xr_ref[...]
    cw = cw_ref[...]
    u = cb_ref[...] + cw[CONV_WIDTH - 1:CONV_WIDTH] * x
    for d in range(CONV_WIDTH - 1):
        u = u + cw[d:d + 1] * cbuf_ref[d]
    a, b = _lru_gates(u, wg_ref, bg_ref, lam_ref)
    h = a * h0_ref[...] + b
    h_ref[...] = h
    y_ref[...] = (h * jax.nn.gelu(gr_ref[...])).astype(y_ref.dtype)


def _lru_step_call(xr, gr, cbuf_t, h0, conv_w, conv_b, wg_bf, bg, lam, name):
    b = xr.shape[0]
    return pl.pallas_call(
        _lru_step_kernel,
        out_shape=[jax.ShapeDtypeStruct((b, D_LRU), BF16),
                   jax.ShapeDtypeStruct((b, D_LRU), F32)],
        compiler_params=pltpu.CompilerParams(vmem_limit_bytes=VMEM_LIMIT_BYTES),
        name=name,
    )(xr, gr, cbuf_t, h0, conv_w, conv_b, wg_bf, bg, lam)


def _outffn_kernel(x_ref, oa_ref, yl_ref, gtm_ref, shf_ref, scf_ref, gtf_ref,
                   gpm_ref, gpf_ref, gqf_ref, wo_ref, wi_ref, wf_ref, o_ref):
    d_ff = wf_ref.shape[0]
    tm = x_ref.shape[0]
    n_groups = FFN_ROW_GROUPS if tm % (FFN_ROW_GROUPS * BF16_ROWS) == 0 else 1
    rows = [pl.ds(g * (tm // n_groups), tm // n_groups) for g in range(n_groups)]
    per_tok = lambda ref, r: ref[r, :] if ref.shape[0] == tm else ref[...]

    mix = [_dot(oa_ref[r, :], wo_ref[0:D_ATT, :]) + _dot(yl_ref[r, :], wo_ref[D_ATT:D_MODEL, :])
           for r in rows]
    x1 = [x_ref[r, :] + per_tok(gtm_ref, r) * _rms(m, gpm_ref[...]) for r, m in zip(rows, mix)]
    h = [(_rms(x, gpf_ref[...]) * (1.0 + per_tok(scf_ref, r)) + per_tok(shf_ref, r)).astype(BF16)
         for r, x in zip(rows, x1)]
    act = []
    for hg in h:
        gate = _dot(hg, wi_ref[:, 0:d_ff])
        up = _dot(hg, wi_ref[:, d_ff:2 * d_ff])
        act.append((gate * jax.nn.sigmoid(gate) * up).astype(BF16))
    f = [_dot(a, wf_ref[...]) for a in act]
    for r, x, fg in zip(rows, x1, f):
        o_ref[r, :] = x + per_tok(gtf_ref, r) * _rms(fg, gqf_ref[...])


def _outffn_call(x, oa, yl, gtm, shf, scf, gtf, gpm, gpf, gqf, wo_bf, wi_bf, wf_bf, tm, layer, name):
    b, s, _ = x.shape
    r = gtm.shape[1]
    rb = 1 if r == 1 else tm
    d_ff = wf_bf.shape[1]
    tok = lambda i, j: (i, j, 0)
    slab = lambda i, j: (layer, 0, 0)
    mod = (lambda i, j: (i, 0, 0)) if r == 1 else tok
    const = lambda i, j: (0, 0)
    resident = pl.Buffered(1)
    return pl.pallas_call(
        _outffn_kernel,
        grid=(b, s // tm),
        in_specs=[
            pl.BlockSpec((None, tm, D_MODEL), tok),
            pl.BlockSpec((None, tm, D_ATT), tok),
            pl.BlockSpec((None, tm, D_LRU), tok),
            pl.BlockSpec((None, rb, D_MODEL), mod),
            pl.BlockSpec((None, rb, D_MODEL), mod),
            pl.BlockSpec((None, rb, D_MODEL), mod),
            pl.BlockSpec((None, rb, D_MODEL), mod),
            pl.BlockSpec((1, D_MODEL), const),
            pl.BlockSpec((1, D_MODEL), const),
            pl.BlockSpec((1, D_MODEL), const),
            pl.BlockSpec((None, D_MODEL, D_MODEL), slab, pipeline_mode=resident),
            pl.BlockSpec((None, D_MODEL, 2 * d_ff), slab, pipeline_mode=resident),
            pl.BlockSpec((None, d_ff, D_MODEL), slab, pipeline_mode=resident),
        ],
        out_specs=pl.BlockSpec((None, tm, D_MODEL), tok),
        out_shape=jax.ShapeDtypeStruct((b, s, D_MODEL), F32),
        compiler_params=_cparams(2),
        name=name,
    )(x, oa, yl, gtm, shf, scf, gtf, gpm, gpf, gqf, wo_bf, wi_bf, wf_bf)


def _block_diag(w):
    n, i, j = w.shape
    eye = jnp.eye(n, dtype=w.dtype)
    return (eye[:, None, :, None] * w[:, :, None, :]).reshape(n * i, n * j)


def kernel(x_prompt, x_sample, cache_k, cache_v, state_h, state_conv, page_table, c_prompt, c_sample,
           w_ada, b_ada, g_pre_mix, w_in, lam_q1, lam_k1, lam_q2, lam_k2, g_subln, conv_w, conv_b,
           lru_wa, lru_ba, lru_wx, lru_bx, lru_lambda, w_out, g_post_mix, g_pre_ffn, w_ffn_in,
           w_ffn_out, g_post_ffn):
    depth = w_ada.shape[0]
    bp, seq, _ = x_prompt.shape
    bs = x_sample.shape[0]
    page = cache_k.shape[2]
    past = page_table.shape[1] * page

    pad = (-(bs + bp)) % SUBLANES
    c_all = jnp.concatenate([c_sample, c_prompt, jnp.zeros((pad, D_MODEL), F32)], axis=0)
    mod = _ada_call(c_all, w_ada, b_ada)

    ys = x_sample.reshape(1, bs, D_MODEL)
    yp = x_prompt

    slope_of = lambda h: 2.0 ** (-8.0 * (h + 1) / N_HEADS)
    slopes = jnp.asarray([slope_of(h) for h in range(N_HEADS)], F32)
    dec_bias = np.zeros((past, LANES), np.float32)
    for j in range(2 * N_HEADS):
        dec_bias[:, j] = -slope_of(j % N_HEADS) * (past - np.arange(past, dtype=np.float32))
    dec_bias = jnp.asarray(dec_bias)
    ck = cache_k.reshape(depth, cache_k.shape[1], page * N_HEADS, D_V)
    cv = cache_v.reshape(depth, cache_v.shape[1], page * N_HEADS, D_V)

    w_in_bf, wo_bf, wi_bf, wf_bf = (w.astype(BF16) for w in (w_in, w_out, w_ffn_in, w_ffn_out))

    outs = {n: [] for n in ("hp", "cp", "ks", "vs", "hs", "cs")}
    kv_stack = ()
    for l in range(depth):
        lam_init = 0.8 - 0.6 * math.exp(-0.3 * l)
        lamv = jnp.stack([lam_q1[l], lam_k1[l], lam_q2[l], lam_k2[l]])
        g_sub = g_subln[l].reshape(1, D_V)
        wg_bf = jnp.concatenate([_block_diag(lru_wa[l]), _block_diag(lru_wx[l])], axis=1).astype(BF16)
        bg = jnp.concatenate([lru_ba[l].reshape(1, D_LRU), lru_bx[l].reshape(1, D_LRU)], axis=1)
        lam_row = lru_lambda[l].reshape(1, D_LRU)
        cb_row = conv_b[l].reshape(1, D_LRU)
        row = lambda v: v[l].reshape(1, D_MODEL)

        mods = [mod[l, :, i * D_MODEL:(i + 1) * D_MODEL] for i in range(6)]
        mod_s = [m[:bs].reshape(1, bs, D_MODEL) for m in mods]
        mod_p = [m[bs:bs + bp].reshape(bp, 1, D_MODEL) for m in mods]

        qt, kp_all, kb, vp_all, vt, xr, gr = _inproj_call(
            yp, mod_p[0], mod_p[1], row(g_pre_mix), w_in_bf, TM_PROJ, True, f"inproj_p{l}",
            layer=l, depth=depth, kv_stack=kv_stack)
        kv_stack = (kp_all, vp_all)
        oa = _attn_call(slopes, qt, kb, vt, lamv, g_sub, lam_init, f"attn_p{l}")
        yl, h8, x8 = _lru_seq_call(xr, gr, conv_w[l], cb_row, wg_bf, bg, lam_row, f"lru_p{l}")
        yp = _outffn_call(yp, oa, yl, mod_p[2], mod_p[3], mod_p[4], mod_p[5],
                          row(g_post_mix), row(g_pre_ffn), row(g_post_ffn),
                          wo_bf, wi_bf, wf_bf, TM_FFN, l, f"outffn_p{l}")
        outs["hp"].append(h8[:, SUBLANES - 1])
        outs["cp"].append(x8[:, SUBLANES - (CONV_WIDTH - 1):])

        q4, k4, v4, xr, gr = _inproj_call(ys, mod_s[0], mod_s[1], row(g_pre_mix), w_in_bf,
                                          bs, False, f"inproj_s{l}", layer=l)
        per_token = lambda a: a.reshape(bs, N_HEADS, D_V)
        oa = _decode_attn_call(page_table, per_token(q4), per_token(k4), per_token(v4), dec_bias, lamv, g_sub,
                               ck, cv, l, lam_init, f"attn_s{l}")
        yl, h_new = _lru_step_call(xr[0], gr[0], jnp.swapaxes(state_conv[l], 0, 1), state_h[l],
                                   conv_w[l], cb_row, wg_bf, bg, lam_row, f"lru_s{l}")
        ys = _outffn_call(ys, oa.reshape(1, bs, D_ATT), yl.reshape(1, bs, D_LRU),
                          mod_s[2], mod_s[3], mod_s[4], mod_s[5],
                          row(g_post_mix), row(g_pre_ffn), row(g_post_ffn),
                          wo_bf, wi_bf, wf_bf, bs, l, f"outffn_s{l}")
        outs["ks"].append(k4.reshape(bs, 1, N_HEADS, D_V))
        outs["vs"].append(v4.reshape(bs, 1, N_HEADS, D_V))
        outs["hs"].append(h_new)
        outs["cs"].append(jnp.concatenate([state_conv[l][:, 1:], xr[0][:, None, :]], axis=1))

    st = {n: jnp.stack(v) for n, v in outs.items()}
    kp_all, vp_all = (a.reshape(depth, bp, seq, N_HEADS, D_V) for a in kv_stack)
    return (yp, ys.reshape(bs, 1, D_MODEL), kp_all, vp_all, st["hp"], st["cp"],
            st["ks"], st["vs"], st["hs"], st["cs"])
```

```python
import functools
import math

import numpy as np
import jax
import jax.numpy as jnp
from jax import lax
from jax.experimental import pallas as pl
from jax.experimental.pallas import tpu as pltpu

D_MODEL = 1024
N_HEADS = 4
D_QK = 64
D_V = 2 * D_QK
D_ATT = N_HEADS * D_V
D_LRU = D_MODEL - D_ATT
N_LRU_BLOCKS = 8
CONV_WIDTH = 4
LRU_C = 8.0
D_IN = 3 * D_ATT + 2 * D_LRU
NORM_EPS = 1e-6
QK_SCALE = D_QK ** -0.5

F32 = jnp.float32
BF16 = jnp.bfloat16

VMEM_LIMIT_BYTES = 56 * 1024 * 1024

SUBLANES = 8
BF16_ROWS = 16
LANES = 128

TM_PROJ = 512
TM_FFN = 512
TQ = TM_PROJ
ATTN_CHUNK = 512
FFN_ROW_GROUPS = 2
T_LRU = 512
DECODE_SLOTS = 4
DECODE_AHEAD = 2
ADA_TN = 1536
BF16_EXACT_INT = 256


def _cparams(n_axes):
    return pltpu.CompilerParams(
        dimension_semantics=("arbitrary",) * n_axes,
        vmem_limit_bytes=VMEM_LIMIT_BYTES)


def _rms(x, g):
    return x * lax.rsqrt(jnp.mean(x * x, axis=-1, keepdims=True) + NORM_EPS) * g


def _dot(a, b):
    return jnp.dot(a, b, preferred_element_type=F32)


def _dot_nt(a, b):
    return lax.dot_general(a, b, (((1,), (1,)), ((), ())), preferred_element_type=F32)


def _ada_kernel(c_ref, w_ref, b_ref, o_ref):
    c = c_ref[...]
    h = (c * jax.nn.sigmoid(c)).astype(BF16)
    o_ref[...] = _dot(h, w_ref[...].astype(BF16)) + b_ref[...]


def _ada_call(c_all, w_ada, b_ada):
    depth, _, n = w_ada.shape
    rows = c_all.shape[0]
    return pl.pallas_call(
        _ada_kernel,
        grid=(depth, n // ADA_TN),
        in_specs=[
            pl.BlockSpec((rows, D_MODEL), lambda l, j: (0, 0)),
            pl.BlockSpec((None, D_MODEL, ADA_TN), lambda l, j: (l, 0, j)),
            pl.BlockSpec((None, 1, ADA_TN), lambda l, j: (l, 0, j)),
        ],
        out_specs=pl.BlockSpec((None, rows, ADA_TN), lambda l, j: (l, 0, j)),
        out_shape=jax.ShapeDtypeStruct((depth, rows, n), F32),
        compiler_params=_cparams(2),
        name="ada_mod",
    )(c_all, w_ada, b_ada.reshape(depth, 1, n))


def _store_heads(ref, x):
    tm = x.shape[0]
    for h in range(N_HEADS):
        ref[pl.ds(h, tm, stride=N_HEADS), :] = x[:, h * D_V:(h + 1) * D_V]


def _inproj_kernel(x_ref, sh_ref, sc_ref, g_ref, w_ref, *rest, prompt, layer, depth):
    x = x_ref[...]
    h = (_rms(x, g_ref[...]) * (1.0 + sc_ref[...]) + sh_ref[...]).astype(BF16)
    q = _dot(h, w_ref[:, 0:D_ATT]) * QK_SCALE
    k = _dot(h, w_ref[:, D_ATT:2 * D_ATT])
    v = _dot(h, w_ref[:, 2 * D_ATT:3 * D_ATT])
    if prompt:
        qt_ref, k4_ref, kb_ref, v4_ref, vt_ref, xr_ref, gr_ref = rest[-7:]
        qt_ref[...] = q.T.astype(BF16)
        for hd in range(N_HEADS):
            kb_ref[hd] = k[:, hd * D_V:(hd + 1) * D_V].astype(BF16)
        vt_ref[...] = v.T.astype(BF16)
        if layer == 0:
            for ref in (k4_ref, v4_ref):
                for d in range(1, depth):
                    ref[d] = jnp.zeros(ref.shape[1:], ref.dtype)
            k4_ref, v4_ref = k4_ref.at[0], v4_ref.at[0]
    else:
        q4_ref, k4_ref, v4_ref, xr_ref, gr_ref = rest
        _store_heads(q4_ref, q)
    _store_heads(k4_ref, k)
    _store_heads(v4_ref, v)
    xr_ref[...] = _dot(h, w_ref[:, 3 * D_ATT:3 * D_ATT + D_LRU])
    gr_ref[...] = _dot(h, w_ref[:, 3 * D_ATT + D_LRU:D_IN])


def _inproj_call(x, sh, sc, g, w_bf, tm, prompt, name, layer=0, depth=1, kv_stack=()):
    b, s, _ = x.shape
    nt = s // tm
    tok = lambda i, j: (i, j, 0)
    tok4 = lambda i, j: (i, j, 0, 0)
    heads = (jax.ShapeDtypeStruct((b, s * N_HEADS, D_V), F32),
             pl.BlockSpec((None, tm * N_HEADS, D_V), tok))
    stack_shape = jax.ShapeDtypeStruct((depth, b, s * N_HEADS, D_V), F32)
    if layer == 0:
        stack = (stack_shape, pl.BlockSpec((depth, None, tm * N_HEADS, D_V), lambda i, j: (0, i, j, 0)))
    else:
        stack = (stack_shape, pl.BlockSpec((None, None, tm * N_HEADS, D_V), lambda i, j: (layer, i, j, 0)))
    flat32 = (jax.ShapeDtypeStruct((b, s, D_LRU), F32), pl.BlockSpec((None, tm, D_LRU), tok))
    flat16 = (jax.ShapeDtypeStruct((b, N_HEADS, s, D_V), BF16),
              pl.BlockSpec((None, N_HEADS, tm, D_V), lambda i, j: (i, 0, j, 0)))
    tiles_t = (jax.ShapeDtypeStruct((b, nt, D_ATT, tm), BF16),
               pl.BlockSpec((None, None, D_ATT, tm), tok4))
    if prompt:
        outs = [tiles_t, stack, flat16, stack, tiles_t, flat32, flat32]
    else:
        outs = [heads, heads, heads, flat32, flat32]
    n_in = 5
    aliases = {n_in + n: out for n, out in zip(range(len(kv_stack)), (1, 3))}
    return pl.pallas_call(
        functools.partial(_inproj_kernel, prompt=prompt, layer=layer, depth=depth),
        grid=(b, nt),
        in_specs=[
            pl.BlockSpec((None, tm, D_MODEL), tok),
            sh[1],
            sc[1],
            g[1],
            pl.BlockSpec((None, D_MODEL, D_IN), lambda i, j: (layer, 0, 0)),
        ] + [pl.BlockSpec(memory_space=pl.ANY) for _ in kv_stack],
        out_specs=[o[1] for o in outs],
        out_shape=[o[0] for o in outs],
        input_output_aliases=aliases,
        compiler_params=_cparams(2),
        name=name,
    )(x, sh[0], sc[0], g[0], w_bf, *kv_stack)


def _diff_lambda(lam_ref, lam_init):
    lv = lam_ref[...]
    s1 = jnp.sum(lv[0:1] * lv[1:2], axis=-1, keepdims=True)
    s2 = jnp.sum(lv[2:3] * lv[3:4], axis=-1, keepdims=True)
    return jnp.exp(s1) - jnp.exp(s2) + lam_init


def _alibi_terms(tq):
    off = np.arange(tq)
    lo = (off % BF16_EXACT_INT).astype(np.float32)
    hi = (off - off % BF16_EXACT_INT).astype(np.float32)
    kterm = np.zeros((N_HEADS, tq, LANES), np.float32)
    qterm = np.zeros((N_HEADS, LANES, 2 * tq), np.float32)
    for h in range(N_HEADS):
        slope = 2.0 ** (-8.0 * (h + 1) / N_HEADS)
        kterm[h, :, 0] = slope * lo
        kterm[h, :, 1] = slope * hi
        kterm[h, :, 2:4] = 1.0
        qterm[h, 0:2, :] = 1.0
        qterm[h, 2, :] = np.tile(-slope * lo, 2)
        qterm[h, 3, :] = np.tile(-slope * hi, 2)
    return jnp.asarray(kterm, BF16), jnp.asarray(qterm, BF16)


def _attn_kernel(slope_ref, qt_ref, k_ref, vt_ref, kterm_ref, qterm_ref, lam_ref, g_ref, o_ref,
                 m_sc, acc_sc, s_sc, smax_sc, *, lam_init):
    h = pl.program_id(1)
    qi = pl.program_id(2)
    slope = slope_ref[h]
    tq = TQ

    qt = qt_ref[...]
    zq = jnp.zeros((D_QK, tq), BF16)
    feat = jnp.concatenate(
        [jnp.concatenate([qt[0:D_QK], zq], axis=0), jnp.concatenate([zq, qt[D_QK:D_V]], axis=0)],
        axis=1)
    rhs = jnp.concatenate([feat, qterm_ref[...]], axis=0)
    k_terms = kterm_ref[...]

    ones = jnp.ones((BF16_ROWS, tq), BF16)

    m_sc[...] = jnp.full(m_sc.shape, -jnp.inf, F32)
    acc_sc[...] = jnp.zeros(acc_sc.shape, F32)

    n_chunks = 2 * tq // ATTN_CHUNK
    chunk = lambda ci: slice(ci * ATTN_CHUNK, (ci + 1) * ATTN_CHUNK)

    def scores(kj, ci):
        start = pl.multiple_of(kj * tq, tq)
        lhs = jnp.concatenate([k_ref[pl.ds(start, tq), :], k_terms], axis=1)
        return _dot(lhs, rhs[:, chunk(ci)])

    def put_scores(buf, ci, kj):
        s = scores(kj, ci)
        s_sc[buf, ci] = s
        smax_sc[buf, ci] = jnp.max(s, axis=0, keepdims=True)

    def block(kj, rd, masked, last):
        vt = jnp.concatenate([vt_ref[kj], ones], axis=0)
        off = slope * ((qi - kj) * tq).astype(F32)
        for ci in range(n_chunks):
            cs = chunk(ci)
            s_ci = s_sc[rd, ci]
            if masked:
                key = lax.broadcasted_iota(jnp.int32, s_ci.shape, 0)
                qry = (lax.broadcasted_iota(jnp.int32, s_ci.shape, 1) + ci * ATTN_CHUNK) & (tq - 1)
                s_ci = jnp.where(qry >= key, s_ci, -jnp.inf)
            s_max = jnp.max(s_ci, axis=0, keepdims=True) if masked else smax_sc[rd, ci]
            m_prev = m_sc[:, cs]
            m_new = jnp.maximum(m_prev, s_max - off)
            p = jnp.exp(s_ci - (m_new + off)).astype(BF16)
            alpha = jnp.exp(m_prev - m_new)
            if not last:
                put_scores(1 - rd, ci, kj + 1)
            acc_sc[:, cs] = alpha * acc_sc[:, cs] + _dot(vt, p)
            m_sc[:, cs] = m_new

    for ci in range(n_chunks):
        put_scores(0, ci, 0)

    def pair(i, carry):
        block(2 * i, 0, masked=False, last=False)
        block(2 * i + 1, 1, masked=False, last=False)
        return carry

    lax.fori_loop(0, lax.shift_right_logical(qi, 1), pair, 0)

    @pl.when((qi & 1) == 1)
    def _():
        block(qi - 1, 0, masked=False, last=False)
        block(qi, 1, masked=True, last=True)

    @pl.when((qi & 1) == 0)
    def _():
        block(qi, 0, masked=True, last=True)

    lam = _diff_lambda(lam_ref, lam_init)
    acc = acc_sc[...]
    o1 = acc[0:D_V, 0:tq] / acc[D_V:D_V + 1, 0:tq]
    o2 = acc[0:D_V, tq:2 * tq] / acc[D_V:D_V + 1, tq:2 * tq]
    o = o1 - lam * o2
    o = o * lax.rsqrt(jnp.mean(o * o, axis=0, keepdims=True) + NORM_EPS)
    o_ref[...] = (o.T * g_ref[...] * (1.0 - lam_init)).astype(o_ref.dtype)


def _attn_call(slopes, qt, k_bf, vt, lamv, g_subln, lam_init, name):
    b, nt = qt.shape[0], qt.shape[1]
    s = nt * TQ
    const = lambda i, h, j: (0, 0)
    return pl.pallas_call(
        functools.partial(_attn_kernel, lam_init=lam_init),
        grid=(b, N_HEADS, nt),
        in_specs=[
            pl.BlockSpec(memory_space=pltpu.SMEM),
            pl.BlockSpec((None, None, D_V, TQ), lambda i, h, j: (i, j, h, 0)),
            pl.BlockSpec((None, None, s, D_V), lambda i, h, j: (i, h, 0, 0)),
            pl.BlockSpec((None, nt, D_V, TQ), lambda i, h, j: (i, 0, h, 0)),
            pl.BlockSpec((None, TQ, LANES), lambda i, h, j: (h, 0, 0)),
            pl.BlockSpec((None, LANES, 2 * TQ), lambda i, h, j: (h, 0, 0)),
            pl.BlockSpec((4, D_QK), const),
            pl.BlockSpec((1, D_V), const),
        ],
        out_specs=pl.BlockSpec((None, TQ, D_V), lambda i, h, j: (i, j, h)),
        out_shape=jax.ShapeDtypeStruct((b, s, D_ATT), BF16),
        scratch_shapes=[
            pltpu.VMEM((1, 2 * TQ), F32),
            pltpu.VMEM((D_V + BF16_ROWS, 2 * TQ), F32),
            pltpu.VMEM((2, 2 * TQ // ATTN_CHUNK, TQ, ATTN_CHUNK), F32),
            pltpu.VMEM((2, 2 * TQ // ATTN_CHUNK, 1, ATTN_CHUNK), F32),
        ],
        compiler_params=_cparams(3),
        name=name,
    )(slopes, qt, k_bf, vt, *_alibi_terms(TQ), lamv, g_subln)


def _dot_tn(a, b):
    return lax.dot_general(a, b, (((0,), (0,)), ((), ())), preferred_element_type=F32)


def _decode_one(q_ref, kn_ref, vn_ref, bias_ref, lam_ref, g_ref, k_pages, v_pages, o_ref, lam_init):
    page = k_pages[0].shape[0] // N_HEADS
    cols = BF16_ROWS

    def flat(ref):
        return jnp.concatenate([ref[h:h + 1, :] for h in range(N_HEADS)], axis=1)

    def dense(ref):
        return jnp.concatenate(
            [ref[pl.ds(h, page, stride=N_HEADS), :] for h in range(N_HEADS)], axis=1).astype(BF16)

    def first_row(x):
        r = lax.broadcasted_iota(jnp.int32, (BF16_ROWS, D_ATT), 0)
        return jnp.where(r == 0, jnp.broadcast_to(x, (BF16_ROWS, D_ATT)), 0.0).astype(BF16)

    r = lax.broadcasted_iota(jnp.int32, (LANES, D_ATT), 0)
    c = lax.broadcasted_iota(jnp.int32, (LANES, D_ATT), 1)
    col = ((c >> 6) & 1) * N_HEADS + (c >> 7)
    qt = jnp.where(r == col, jnp.broadcast_to(flat(q_ref), (LANES, D_ATT)), 0.0).astype(BF16)

    s = jnp.concatenate([_dot_nt(dense(kp), qt) for kp in k_pages], axis=0)
    s = s + bias_ref[...]
    s_new = _dot_nt(first_row(flat(kn_ref)), qt)
    s_new = jnp.where(lax.broadcasted_iota(jnp.int32, s_new.shape, 0) == 0, s_new, -jnp.inf)

    m = jnp.maximum(jnp.max(s, axis=0, keepdims=True), jnp.max(s_new, axis=0, keepdims=True))
    e = jnp.exp(s - m)
    e_new = jnp.exp(s_new - m)
    l = jnp.sum(e, axis=0, keepdims=True) + jnp.sum(e_new, axis=0, keepdims=True)
    p = (e / l)[:, 0:cols].astype(BF16)
    p_new = (e_new / l)[:, 0:cols].astype(BF16)

    acc = _dot_tn(p_new, first_row(flat(vn_ref)))
    for i, vp in enumerate(v_pages):
        acc = acc + _dot_tn(p[i * page:(i + 1) * page], dense(vp))

    lam = _diff_lambda(lam_ref, lam_init)
    for h in range(N_HEADS):
        o = (acc[h:h + 1, h * D_V:(h + 1) * D_V]
             - lam * acc[N_HEADS + h:N_HEADS + h + 1, h * D_V:(h + 1) * D_V])
        o_ref[:, h * D_V:(h + 1) * D_V] = (_rms(o, g_ref[...]) * (1.0 - lam_init)).astype(o_ref.dtype)


def _decode_attn_kernel(pt_ref, q_ref, kn_ref, vn_ref, bias_ref, lam_ref, g_ref, ck_hbm, cv_hbm,
                        o_ref, kbuf, vbuf, sem, *, layer, n_pages, lam_init):
    i = pl.program_id(0)
    n = pl.num_programs(0)

    def page_copies(seq, slot):
        out = []
        for j in range(n_pages):
            pg = pt_ref[seq, j]
            out.append(pltpu.make_async_copy(ck_hbm.at[layer, pg], kbuf.at[slot, j], sem.at[0, slot]))
            out.append(pltpu.make_async_copy(cv_hbm.at[layer, pg], vbuf.at[slot, j], sem.at[1, slot]))
        return out

    def start(seq, slot):
        for cp in page_copies(seq, slot):
            cp.start()

    def wait(seq, slot):
        for cp in page_copies(seq, slot):
            cp.wait()

    def compute(slot):
        _decode_one(q_ref.at[slot], kn_ref.at[slot], vn_ref.at[slot], bias_ref, lam_ref, g_ref,
                    [kbuf.at[slot, j] for j in range(n_pages)],
                    [vbuf.at[slot, j] for j in range(n_pages)], o_ref.at[slot], lam_init)

    per_step = kbuf.shape[0]

    @pl.when(i == 0)
    def _():
        for s in range(DECODE_AHEAD):
            start(s, s)

    for k in range(per_step):
        nxt = k + DECODE_AHEAD
        if nxt < per_step:
            start(per_step * i + nxt, nxt)
        else:
            @pl.when(i + 1 < n)
            def _(nxt=nxt):
                start(per_step * i + nxt, nxt - per_step)
        wait(per_step * i + k, k)
        compute(k)


def _decode_attn_call(page_table, q4, k4, v4, bias, lamv, g_subln, cache_k, cache_v,
                      layer, lam_init, name):
    b = q4.shape[0]
    n_pages = page_table.shape[1]
    slab = cache_k.shape[2]
    per_step = DECODE_SLOTS
    tok = lambda i, pt: (i, 0, 0)
    const = lambda i, pt: (0, 0)
    grid_spec = pltpu.PrefetchScalarGridSpec(
        num_scalar_prefetch=1,
        grid=(b // per_step,),
        in_specs=[
            pl.BlockSpec((per_step, N_HEADS, D_V), tok),
            pl.BlockSpec((per_step, N_HEADS, D_V), tok),
            pl.BlockSpec((per_step, N_HEADS, D_V), tok),
            pl.BlockSpec(bias.shape, const),
            pl.BlockSpec((4, D_QK), const),
            pl.BlockSpec((1, D_V), const),
            pl.BlockSpec(memory_space=pl.ANY),
            pl.BlockSpec(memory_space=pl.ANY),
        ],
        out_specs=pl.BlockSpec((per_step, 1, D_ATT), tok),
        scratch_shapes=[
            pltpu.VMEM((per_step, n_pages, slab, D_V), F32),
            pltpu.VMEM((per_step, n_pages, slab, D_V), F32),
            pltpu.SemaphoreType.DMA((2, per_step)),
        ],
    )
    return pl.pallas_call(
        functools.partial(_decode_attn_kernel, layer=layer, n_pages=n_pages, lam_init=lam_init),
        grid_spec=grid_spec,
        out_shape=jax.ShapeDtypeStruct((b, 1, D_ATT), BF16),
        compiler_params=_cparams(1),
        name=name,
    )(page_table, q4, k4, v4, bias, lamv, g_subln, cache_k, cache_v)


def _lru_gates(u, wg_ref, bg_ref, lam_ref):
    gates = jax.nn.sigmoid(_dot(u.astype(BF16), wg_ref[...]) + bg_ref[...])
    r = gates[:, :D_LRU]
    i_g = gates[:, D_LRU:]
    neg_lam = -lam_ref[...]
    softplus = jnp.maximum(neg_lam, 0.0) + jnp.log1p(jnp.exp(-jnp.abs(neg_lam)))
    log_a = -LRU_C * r * softplus
    a = jnp.exp(log_a)
    one_minus_a2 = -jnp.tanh(log_a) * (a * a + 1.0)
    root = jnp.where(one_minus_a2 > 0.0, one_minus_a2 * lax.rsqrt(one_minus_a2), 0.0)
    return a, root * (i_g * u)


def _lru_seq_kernel(xr_ref, gr_ref, cw_ref, cb_ref, wg_ref, bg_ref, lam_ref,
                    y_ref, h_ref, tail_ref, xbuf, hbuf):
    t = T_LRU
    j = pl.program_id(1)

    @pl.when(j == 0)
    def _():
        xbuf[0:SUBLANES, :] = jnp.zeros((SUBLANES, D_LRU), F32)
        hbuf[...] = jnp.zeros(hbuf.shape, F32)

    x = xr_ref[...]
    xbuf[SUBLANES:SUBLANES + t, :] = x
    cw = cw_ref[...]
    u = cb_ref[...] + cw[CONV_WIDTH - 1:CONV_WIDTH] * x
    for d in range(1, CONV_WIDTH):
        u = u + cw[CONV_WIDTH - 1 - d:CONV_WIDTH - d] * xbuf[SUBLANES - d:SUBLANES - d + t, :]

    a, b = _lru_gates(u, wg_ref, bg_ref, lam_ref)

    groups = t // SUBLANES
    a = a.reshape(groups, SUBLANES, D_LRU)
    b = b.reshape(groups, SUBLANES, D_LRU)
    row = lax.broadcasted_iota(jnp.int32, a.shape, 1)
    shift = 1
    while shift < SUBLANES:
        a_prev = pltpu.roll(a, shift, 1)
        b_prev = pltpu.roll(b, shift, 1)
        live = row >= shift
        b = jnp.where(live, a * b_prev + b, b)
        a = jnp.where(live, a * a_prev, a)
        shift *= 2

    h = hbuf[SUBLANES - 1:SUBLANES, :]
    rows = []
    for g in range(groups):
        rows.append(b[g] + a[g] * h)
        h = rows[-1][SUBLANES - 1:SUBLANES, :]
    hs = jnp.concatenate(rows, axis=0)
    y_ref[...] = (hs * jax.nn.gelu(gr_ref[...])).astype(y_ref.dtype)

    hbuf[...] = hs[t - SUBLANES:t]
    xbuf[0:SUBLANES, :] = x[t - SUBLANES:t]
    h_ref[...] = hs[t - SUBLANES:t]
    tail_ref[...] = x[t - SUBLANES:t]


def _lru_seq_call(xr, gr, conv_w, conv_b, wg_bf, bg, lam, name):
    b, s, _ = xr.shape
    tok = lambda i, j: (i, j, 0)
    const = lambda i, j: (0, 0)
    per_seq = lambda i, j: (i, 0, 0)
    return pl.pallas_call(
        _lru_seq_kernel,
        grid=(b, s // T_LRU),
        in_specs=[
            pl.BlockSpec((None, T_LRU, D_LRU), tok),
            pl.BlockSpec((None, T_LRU, D_LRU), tok),
            pl.BlockSpec((CONV_WIDTH, D_LRU), const),
            pl.BlockSpec((1, D_LRU), const),
            pl.BlockSpec((D_LRU, 2 * D_LRU), const),
            pl.BlockSpec((1, 2 * D_LRU), const),
            pl.BlockSpec((1, D_LRU), const),
        ],
        out_specs=[
            pl.BlockSpec((None, T_LRU, D_LRU), tok),
            pl.BlockSpec((None, SUBLANES, D_LRU), per_seq),
            pl.BlockSpec((None, SUBLANES, D_LRU), per_seq),
        ],
        out_shape=[
            jax.ShapeDtypeStruct((b, s, D_LRU), BF16),
            jax.ShapeDtypeStruct((b, SUBLANES, D_LRU), F32),
            jax.ShapeDtypeStruct((b, SUBLANES, D_LRU), F32),
        ],
        scratch_shapes=[
            pltpu.VMEM((SUBLANES + T_LRU, D_LRU), F32),
            pltpu.VMEM((SUBLANES, D_LRU), F32),
        ],
        compiler_params=_cparams(2),
        name=name,
    )(xr, gr, conv_w, conv_b, wg_bf, bg, lam)


def _lru_step_kernel(xr_ref, gr_ref, cbuf_ref, h0_ref, cw_ref, cb_ref, wg_ref, bg_ref, lam_ref,
                     y_ref, h_ref):
    x = xr_ref[...]
    cw = cw_ref[...]
    u = cb_ref[...] + cw[CONV_WIDTH - 1:CONV_WIDTH] * x
    for d in range(CONV_WIDTH - 1):
        u = u + cw[d:d + 1] * cbuf_ref[d]
    a, b = _lru_gates(u, wg_ref, bg_ref, lam_ref)
    h = a * h0_ref[...] + b
    h_ref[...] = h
    y_ref[...] = (h * jax.nn.gelu(gr_ref[...])).astype(y_ref.dtype)


def _lru_step_call(xr, gr, cbuf_t, h0, conv_w, conv_b, wg_bf, bg, lam, name):
    b = xr.shape[0]
    return pl.pallas_call(
        _lru_step_kernel,
        out_shape=[jax.ShapeDtypeStruct((b, D_LRU), BF16),
                   jax.ShapeDtypeStruct((b, D_LRU), F32)],
        compiler_params=pltpu.CompilerParams(vmem_limit_bytes=VMEM_LIMIT_BYTES),
        name=name,
    )(xr, gr, cbuf_t, h0, conv_w, conv_b, wg_bf, bg, lam)


def _outffn_kernel(x_ref, oa_ref, yl_ref, gtm_ref, shf_ref, scf_ref, gtf_ref,
                   gpm_ref, gpf_ref, gqf_ref, wo_ref, wi_ref, wf_ref, o_ref):
    d_ff = wf_ref.shape[0]
    tm = x_ref.shape[0]
    n_groups = FFN_ROW_GROUPS if tm % (FFN_ROW_GROUPS * BF16_ROWS) == 0 else 1
    rows = [pl.ds(g * (tm // n_groups), tm // n_groups) for g in range(n_groups)]
    per_tok = lambda ref, r: ref[r, :] if ref.shape[0] == tm else ref[...]

    mix = [_dot(oa_ref[r, :], wo_ref[0:D_ATT, :]) + _dot(yl_ref[r, :], wo_ref[D_ATT:D_MODEL, :])
           for r in rows]
    x1 = [x_ref[r, :] + per_tok(gtm_ref, r) * _rms(m, gpm_ref[...]) for r, m in zip(rows, mix)]
    h = [(_rms(x, gpf_ref[...]) * (1.0 + per_tok(scf_ref, r)) + per_tok(shf_ref, r)).astype(BF16)
         for r, x in zip(rows, x1)]
    act = []
    for hg in h:
        gate = _dot(hg, wi_ref[:, 0:d_ff])
        up = _dot(hg, wi_ref[:, d_ff:2 * d_ff])
        act.append((gate * jax.nn.sigmoid(gate) * up).astype(BF16))
    f = [_dot(a, wf_ref[...]) for a in act]
    for r, x, fg in zip(rows, x1, f):
        o_ref[r, :] = x + per_tok(gtf_ref, r) * _rms(fg, gqf_ref[...])


def _outffn_call(x, oa, yl, gtm, shf, scf, gtf, gpm, gpf, gqf, wo_bf, wi_bf, wf_bf, tm, layer, name):
    b, s, _ = x.shape
    d_ff = wf_bf.shape[1]
    tok = lambda i, j: (i, j, 0)
    slab = lambda i, j: (layer, 0, 0)
    small = (gtm, shf, scf, gtf, gpm, gpf, gqf)
    resident = pl.Buffered(1)
    return pl.pallas_call(
        _outffn_kernel,
        grid=(b, s // tm),
        in_specs=[
            pl.BlockSpec((None, tm, D_MODEL), tok),
            pl.BlockSpec((None, tm, D_ATT), tok),
            pl.BlockSpec((None, tm, D_LRU), tok),
            *[spec for _, spec in small],
            pl.BlockSpec((None, D_MODEL, D_MODEL), slab, pipeline_mode=resident),
            pl.BlockSpec((None, D_MODEL, 2 * d_ff), slab, pipeline_mode=resident),
            pl.BlockSpec((None, d_ff, D_MODEL), slab, pipeline_mode=resident),
        ],
        out_specs=pl.BlockSpec((None, tm, D_MODEL), tok),
        out_shape=jax.ShapeDtypeStruct((b, s, D_MODEL), F32),
        compiler_params=_cparams(2),
        name=name,
    )(x, oa, yl, *[arr for arr, _ in small], wo_bf, wi_bf, wf_bf)


def _block_diag(w):
    n, i, j = w.shape
    eye = jnp.eye(n, dtype=w.dtype)
    return (eye[:, None, :, None] * w[:, :, None, :]).reshape(n * i, n * j)


def kernel(x_prompt, x_sample, cache_k, cache_v, state_h, state_conv, page_table, c_prompt, c_sample,
           w_ada, b_ada, g_pre_mix, w_in, lam_q1, lam_k1, lam_q2, lam_k2, g_subln, conv_w, conv_b,
           lru_wa, lru_ba, lru_wx, lru_bx, lru_lambda, w_out, g_post_mix, g_pre_ffn, w_ffn_in,
           w_ffn_out, g_post_ffn):
    depth = w_ada.shape[0]
    bp, seq, _ = x_prompt.shape
    bs = x_sample.shape[0]
    page = cache_k.shape[2]
    past = page_table.shape[1] * page

    pad = (-(bs + bp)) % SUBLANES
    c_all = jnp.concatenate([c_sample, c_prompt, jnp.zeros((pad, D_MODEL), F32)], axis=0)
    mod = _ada_call(c_all, w_ada, b_ada)
    mod_rows = mod.reshape(depth, mod.shape[1], 1, mod.shape[2])

    ys = x_sample.reshape(1, bs, D_MODEL)
    yp = x_prompt

    slope_of = lambda h: 2.0 ** (-8.0 * (h + 1) / N_HEADS)
    slopes = jnp.asarray([slope_of(h) for h in range(N_HEADS)], F32)
    dec_bias = np.zeros((past, LANES), np.float32)
    for j in range(2 * N_HEADS):
        dec_bias[:, j] = -slope_of(j % N_HEADS) * (past - np.arange(past, dtype=np.float32))
    dec_bias = jnp.asarray(dec_bias)
    ck = cache_k.reshape(depth, cache_k.shape[1], page * N_HEADS, D_V)
    cv = cache_v.reshape(depth, cache_v.shape[1], page * N_HEADS, D_V)

    w_in_bf, wo_bf, wi_bf, wf_bf = (w.astype(BF16) for w in (w_in, w_out, w_ffn_in, w_ffn_out))

    outs = {n: [] for n in ("hp", "cp", "ks", "vs", "hs", "cs")}
    kv_stack = ()
    for l in range(depth):
        lam_init = 0.8 - 0.6 * math.exp(-0.3 * l)
        lamv = jnp.stack([lam_q1[l], lam_k1[l], lam_q2[l], lam_k2[l]])
        g_sub = g_subln[l].reshape(1, D_V)
        wg_bf = jnp.concatenate([_block_diag(lru_wa[l]), _block_diag(lru_wx[l])], axis=1).astype(BF16)
        bg = jnp.concatenate([lru_ba[l].reshape(1, D_LRU), lru_bx[l].reshape(1, D_LRU)], axis=1)
        lam_row = lru_lambda[l].reshape(1, D_LRU)
        cb_row = conv_b[l].reshape(1, D_LRU)
        row = lambda v: (v.reshape(depth, 1, D_MODEL),
                         pl.BlockSpec((None, 1, D_MODEL), lambda i, j, l=l: (l, 0, 0)))
        mod_s = [(mod, pl.BlockSpec((None, bs, D_MODEL), lambda i, j, c=c, l=l: (l, j, c)))
                 for c in range(6)]
        mod_p = [(mod_rows, pl.BlockSpec((None, None, 1, D_MODEL), lambda i, j, c=c, l=l: (l, bs + i, 0, c)))
                 for c in range(6)]

        qt, kp_all, kb, vp_all, vt, xr, gr = _inproj_call(
            yp, mod_p[0], mod_p[1], row(g_pre_mix), w_in_bf, TM_PROJ, True, f"inproj_p{l}",
            layer=l, depth=depth, kv_stack=kv_stack)
        kv_stack = (kp_all, vp_all)
        oa = _attn_call(slopes, qt, kb, vt, lamv, g_sub, lam_init, f"attn_p{l}")
        yl, h8, x8 = _lru_seq_call(xr, gr, conv_w[l], cb_row, wg_bf, bg, lam_row, f"lru_p{l}")
        yp = _outffn_call(yp, oa, yl, mod_p[2], mod_p[3], mod_p[4], mod_p[5],
                          row(g_post_mix), row(g_pre_ffn), row(g_post_ffn),
                          wo_bf, wi_bf, wf_bf, TM_FFN, l, f"outffn_p{l}")
        outs["hp"].append(h8[:, SUBLANES - 1])
        outs["cp"].append(x8[:, SUBLANES - (CONV_WIDTH - 1):])

        q4, k4, v4, xr, gr = _inproj_call(ys, mod_s[0], mod_s[1], row(g_pre_mix), w_in_bf,
                                          bs, False, f"inproj_s{l}", layer=l)
        per_token = lambda a: a.reshape(bs, N_HEADS, D_V)
        oa = _decode_attn_call(page_table, per_token(q4), per_token(k4), per_token(v4), dec_bias, lamv, g_sub,
                               ck, cv, l, lam_init, f"attn_s{l}")
        yl, h_new = _lru_step_call(xr[0], gr[0], jnp.swapaxes(state_conv[l], 0, 1), state_h[l],
                                   conv_w[l], cb_row, wg_bf, bg, lam_row, f"lru_s{l}")
        ys = _outffn_call(ys, oa.reshape(1, bs, D_ATT), yl.reshape(1, bs, D_LRU),
                          mod_s[2], mod_s[3], mod_s[4], mod_s[5],
                          row(g_post_mix), row(g_pre_ffn), row(g_post_ffn),
                          wo_bf, wi_bf, wf_bf, bs, l, f"outffn_s{l}")
        outs["ks"].append(k4.reshape(bs, 1, N_HEADS, D_V))
        outs["vs"].append(v4.reshape(bs, 1, N_HEADS, D_V))
        outs["hs"].append(h_new)
        outs["cs"].append(jnp.concatenate([state_conv[l][:, 1:], xr[0][:, None, :]], axis=1))

    st = {n: jnp.stack(v) for n, v in outs.items()}
    kp_all, vp_all = (a.reshape(depth, bp, seq, N_HEADS, D_V) for a in kv_stack)
    return (yp, ys.reshape(bs, 1, D_MODEL), kp_all, vp_all, st["hp"], st["cp"],
            st["ks"], st["vs"], st["hs"], st["cs"])
```

```python
import functools
import math

import numpy as np
import jax
import jax.numpy as jnp
from jax import lax
from jax.experimental import pallas as pl
from jax.experimental.pallas import tpu as pltpu

D_MODEL = 1024
N_HEADS = 4
D_QK = 64
D_V = 2 * D_QK
D_ATT = N_HEADS * D_V
D_LRU = D_MODEL - D_ATT
N_LRU_BLOCKS = 8
CONV_WIDTH = 4
LRU_C = 8.0
D_IN = 3 * D_ATT + 2 * D_LRU
NORM_EPS = 1e-6
QK_SCALE = D_QK ** -0.5

F32 = jnp.float32
BF16 = jnp.bfloat16

VMEM_LIMIT_BYTES = 56 * 1024 * 1024

SUBLANES = 8
BF16_ROWS = 16
LANES = 128

TM_PROJ = 512
TM_FFN = 512
TQ = TM_PROJ
ATTN_CHUNK = 512
FFN_ROW_GROUPS = 2
T_LRU = 512
DECODE_SLOTS = 4
DECODE_AHEAD = 2
ADA_TN = 1536
BF16_EXACT_INT = 256


def _cparams(n_axes):
    return pltpu.CompilerParams(
        dimension_semantics=("arbitrary",) * n_axes,
        vmem_limit_bytes=VMEM_LIMIT_BYTES)


def _rms(x, g):
    return x * lax.rsqrt(jnp.mean(x * x, axis=-1, keepdims=True) + NORM_EPS) * g


def _dot(a, b):
    return jnp.dot(a, b, preferred_element_type=F32)


def _dot_nt(a, b):
    return lax.dot_general(a, b, (((1,), (1,)), ((), ())), preferred_element_type=F32)


def _ada_kernel(c_ref, w_ref, b_ref, o_ref):
    c = c_ref[...]
    h = (c * jax.nn.sigmoid(c)).astype(BF16)
    o_ref[...] = _dot(h, w_ref[...].astype(BF16)) + b_ref[...]


def _ada_call(c_all, w_ada, b_ada):
    depth, _, n = w_ada.shape
    rows = c_all.shape[0]
    return pl.pallas_call(
        _ada_kernel,
        grid=(depth, n // ADA_TN),
        in_specs=[
            pl.BlockSpec((rows, D_MODEL), lambda l, j: (0, 0)),
            pl.BlockSpec((None, D_MODEL, ADA_TN), lambda l, j: (l, 0, j)),
            pl.BlockSpec((None, 1, ADA_TN), lambda l, j: (l, 0, j)),
        ],
        out_specs=pl.BlockSpec((None, rows, ADA_TN), lambda l, j: (l, 0, j)),
        out_shape=jax.ShapeDtypeStruct((depth, rows, n), F32),
        compiler_params=_cparams(2),
        name="ada_mod",
    )(c_all, w_ada, b_ada.reshape(depth, 1, n))


def _store_heads(ref, x):
    tm = x.shape[0]
    for h in range(N_HEADS):
        ref[pl.ds(h, tm, stride=N_HEADS), :] = x[:, h * D_V:(h + 1) * D_V]


def _inproj_kernel(x_ref, sh_ref, sc_ref, g_ref, w_ref, *rest, prompt, layer, depth):
    x = x_ref[...]
    h = (_rms(x, g_ref[...]) * (1.0 + sc_ref[...]) + sh_ref[...]).astype(BF16)
    q = _dot(h, w_ref[:, 0:D_ATT]) * QK_SCALE
    k = _dot(h, w_ref[:, D_ATT:2 * D_ATT])
    v = _dot(h, w_ref[:, 2 * D_ATT:3 * D_ATT])
    if prompt:
        qt_ref, k4_ref, kb_ref, v4_ref, vt_ref, xr_ref, gr_ref = rest[-7:]
        qt_ref[...] = q.T.astype(BF16)
        for hd in range(N_HEADS):
            kb_ref[hd] = k[:, hd * D_V:(hd + 1) * D_V].astype(BF16)
        vt_ref[...] = v.T.astype(BF16)
        if layer == 0:
            for ref in (k4_ref, v4_ref):
                for d in range(1, depth):
                    ref[d] = jnp.zeros(ref.shape[1:], ref.dtype)
            k4_ref, v4_ref = k4_ref.at[0], v4_ref.at[0]
    else:
        q4_ref, k4_ref, v4_ref, xr_ref, gr_ref = rest
        _store_heads(q4_ref, q)
    _store_heads(k4_ref, k)
    _store_heads(v4_ref, v)
    xr_ref[...] = _dot(h, w_ref[:, 3 * D_ATT:3 * D_ATT + D_LRU])
    gr_ref[...] = _dot(h, w_ref[:, 3 * D_ATT + D_LRU:D_IN])


def _inproj_call(x, sh, sc, g, w_bf, tm, prompt, name, layer=0, depth=1, kv_stack=()):
    b, s, _ = x.shape
    nt = s // tm
    tok = lambda i, j: (i, j, 0)
    tok4 = lambda i, j: (i, j, 0, 0)
    heads = (jax.ShapeDtypeStruct((b, s * N_HEADS, D_V), F32),
             pl.BlockSpec((None, tm * N_HEADS, D_V), tok))
    stack_shape = jax.ShapeDtypeStruct((depth, b, s * N_HEADS, D_V), F32)
    if layer == 0:
        stack = (stack_shape, pl.BlockSpec((depth, None, tm * N_HEADS, D_V), lambda i, j: (0, i, j, 0)))
    else:
        stack = (stack_shape, pl.BlockSpec((None, None, tm * N_HEADS, D_V), lambda i, j: (layer, i, j, 0)))
    flat32 = (jax.ShapeDtypeStruct((b, s, D_LRU), F32), pl.BlockSpec((None, tm, D_LRU), tok))
    flat16 = (jax.ShapeDtypeStruct((b, N_HEADS, s, D_V), BF16),
              pl.BlockSpec((None, N_HEADS, tm, D_V), lambda i, j: (i, 0, j, 0)))
    tiles_t = (jax.ShapeDtypeStruct((b, nt, D_ATT, tm), BF16),
               pl.BlockSpec((None, None, D_ATT, tm), tok4))
    if prompt:
        outs = [tiles_t, stack, flat16, stack, tiles_t, flat32, flat32]
    else:
        outs = [heads, heads, heads, flat32, flat32]
    n_in = 5
    aliases = {n_in + n: out for n, out in zip(range(len(kv_stack)), (1, 3))}
    return pl.pallas_call(
        functools.partial(_inproj_kernel, prompt=prompt, layer=layer, depth=depth),
        grid=(b, nt),
        in_specs=[
            pl.BlockSpec((None, tm, D_MODEL), tok),
            sh[1],
            sc[1],
            g[1],
            pl.BlockSpec((None, D_MODEL, D_IN), lambda i, j: (layer, 0, 0)),
        ] + [pl.BlockSpec(memory_space=pl.ANY) for _ in kv_stack],
        out_specs=[o[1] for o in outs],
        out_shape=[o[0] for o in outs],
        input_output_aliases=aliases,
        compiler_params=_cparams(2),
        name=name,
    )(x, sh[0], sc[0], g[0], w_bf, *kv_stack)


def _diff_lambda(lam_ref, lam_init):
    lv = lam_ref[...]
    s1 = jnp.sum(lv[0:1] * lv[1:2], axis=-1, keepdims=True)
    s2 = jnp.sum(lv[2:3] * lv[3:4], axis=-1, keepdims=True)
    return jnp.exp(s1) - jnp.exp(s2) + lam_init


def _alibi_terms(tq):
    off = np.arange(tq)
    lo = (off % BF16_EXACT_INT).astype(np.float32)
    hi = (off - off % BF16_EXACT_INT).astype(np.float32)
    kterm = np.zeros((N_HEADS, tq, LANES), np.float32)
    qterm = np.zeros((N_HEADS, LANES, 2 * tq), np.float32)
    for h in range(N_HEADS):
        slope = 2.0 ** (-8.0 * (h + 1) / N_HEADS)
        kterm[h, :, 0] = slope * lo
        kterm[h, :, 1] = slope * hi
        kterm[h, :, 2:4] = 1.0
        qterm[h, 0:2, :] = 1.0
        qterm[h, 2, :] = np.tile(-slope * lo, 2)
        qterm[h, 3, :] = np.tile(-slope * hi, 2)
    return jnp.asarray(kterm, BF16), jnp.asarray(qterm, BF16)


def _attn_kernel(slope_ref, qt_ref, k_ref, vt_ref, kterm_ref, qterm_ref, lam_ref, g_ref, o_ref,
                 m_sc, acc_sc, s_sc, smax_sc, *, lam_init):
    h = pl.program_id(1)
    qi = pl.program_id(2)
    slope = slope_ref[h]
    tq = TQ

    qt = qt_ref[...]
    zq = jnp.zeros((D_QK, tq), BF16)
    feat = jnp.concatenate(
        [jnp.concatenate([qt[0:D_QK], zq], axis=0), jnp.concatenate([zq, qt[D_QK:D_V]], axis=0)],
        axis=1)
    rhs = jnp.concatenate([feat, qterm_ref[...]], axis=0)
    k_terms = kterm_ref[...]

    ones = jnp.ones((BF16_ROWS, tq), BF16)

    m_sc[...] = jnp.full(m_sc.shape, -jnp.inf, F32)
    acc_sc[...] = jnp.zeros(acc_sc.shape, F32)

    n_chunks = 2 * tq // ATTN_CHUNK
    chunk = lambda ci: slice(ci * ATTN_CHUNK, (ci + 1) * ATTN_CHUNK)

    def scores(kj, ci):
        start = pl.multiple_of(kj * tq, tq)
        lhs = jnp.concatenate([k_ref[pl.ds(start, tq), :], k_terms], axis=1)
        return _dot(lhs, rhs[:, chunk(ci)])

    def put_scores(buf, ci, kj):
        s = scores(kj, ci)
        s_sc[buf, ci] = s
        smax_sc[buf, ci] = jnp.max(s, axis=0, keepdims=True)

    def block(kj, rd, masked, last):
        vt = jnp.concatenate([vt_ref[kj], ones], axis=0)
        off = slope * ((qi - kj) * tq).astype(F32)
        for ci in range(n_chunks):
            cs = chunk(ci)
            s_ci = s_sc[rd, ci]
            if masked:
                key = lax.broadcasted_iota(jnp.int32, s_ci.shape, 0)
                qry = (lax.broadcasted_iota(jnp.int32, s_ci.shape, 1) + ci * ATTN_CHUNK) & (tq - 1)
                s_ci = jnp.where(qry >= key, s_ci, -jnp.inf)
            s_max = jnp.max(s_ci, axis=0, keepdims=True) if masked else smax_sc[rd, ci]
            m_prev = m_sc[:, cs]
            m_new = jnp.maximum(m_prev, s_max - off)
            p = jnp.exp(s_ci - (m_new + off)).astype(BF16)
            alpha = jnp.exp(m_prev - m_new)
            if not last:
                put_scores(1 - rd, ci, kj + 1)
            acc_sc[:, cs] = alpha * acc_sc[:, cs] + _dot(vt, p)
            m_sc[:, cs] = m_new

    for ci in range(n_chunks):
        put_scores(0, ci, 0)

    def pair(i, carry):
        block(2 * i, 0, masked=False, last=False)
        block(2 * i + 1, 1, masked=False, last=False)
        return carry

    lax.fori_loop(0, lax.shift_right_logical(qi, 1), pair, 0)

    @pl.when((qi & 1) == 1)
    def _():
        block(qi - 1, 0, masked=False, last=False)
        block(qi, 1, masked=True, last=True)

    @pl.when((qi & 1) == 0)
    def _():
        block(qi, 0, masked=True, last=True)

    lam = _diff_lambda(lam_ref, lam_init)
    acc = acc_sc[...]
    o1 = acc[0:D_V, 0:tq] / acc[D_V:D_V + 1, 0:tq]
    o2 = acc[0:D_V, tq:2 * tq] / acc[D_V:D_V + 1, tq:2 * tq]
    o = o1 - lam * o2
    o = o * lax.rsqrt(jnp.mean(o * o, axis=0, keepdims=True) + NORM_EPS)
    o_ref[...] = (o.T * g_ref[...] * (1.0 - lam_init)).astype(o_ref.dtype)


def _attn_call(slopes, qt, k_bf, vt, lamv, g_subln, lam_init, name):
    b, nt = qt.shape[0], qt.shape[1]
    s = nt * TQ
    const = lambda i, h, j: (0, 0)
    return pl.pallas_call(
        functools.partial(_attn_kernel, lam_init=lam_init),
        grid=(b, N_HEADS, nt),
        in_specs=[
            pl.BlockSpec(memory_space=pltpu.SMEM),
            pl.BlockSpec((None, None, D_V, TQ), lambda i, h, j: (i, j, h, 0)),
            pl.BlockSpec((None, None, s, D_V), lambda i, h, j: (i, h, 0, 0)),
            pl.BlockSpec((None, nt, D_V, TQ), lambda i, h, j: (i, 0, h, 0)),
            pl.BlockSpec((None, TQ, LANES), lambda i, h, j: (h, 0, 0)),
            pl.BlockSpec((None, LANES, 2 * TQ), lambda i, h, j: (h, 0, 0)),
            pl.BlockSpec((4, D_QK), const),
            pl.BlockSpec((1, D_V), const),
        ],
        out_specs=pl.BlockSpec((None, TQ, D_V), lambda i, h, j: (i, j, h)),
        out_shape=jax.ShapeDtypeStruct((b, s, D_ATT), BF16),
        scratch_shapes=[
            pltpu.VMEM((1, 2 * TQ), F32),
            pltpu.VMEM((D_V + BF16_ROWS, 2 * TQ), F32),
            pltpu.VMEM((2, 2 * TQ // ATTN_CHUNK, TQ, ATTN_CHUNK), F32),
            pltpu.VMEM((2, 2 * TQ // ATTN_CHUNK, 1, ATTN_CHUNK), F32),
        ],
        compiler_params=_cparams(3),
        name=name,
    )(slopes, qt, k_bf, vt, *_alibi_terms(TQ), lamv, g_subln)


def _dot_tn(a, b):
    return lax.dot_general(a, b, (((0,), (0,)), ((), ())), preferred_element_type=F32)


def _decode_one(q_ref, kn_ref, vn_ref, bias_ref, lam_ref, g_ref, k_pages, v_pages, o_ref, lam_init):
    page = k_pages[0].shape[0] // N_HEADS
    cols = BF16_ROWS

    def flat(ref):
        return jnp.concatenate([ref[h:h + 1, :] for h in range(N_HEADS)], axis=1)

    def dense(ref):
        return jnp.concatenate(
            [ref[pl.ds(h, page, stride=N_HEADS), :] for h in range(N_HEADS)], axis=1).astype(BF16)

    def first_row(x):
        r = lax.broadcasted_iota(jnp.int32, (BF16_ROWS, D_ATT), 0)
        return jnp.where(r == 0, jnp.broadcast_to(x, (BF16_ROWS, D_ATT)), 0.0).astype(BF16)

    r = lax.broadcasted_iota(jnp.int32, (LANES, D_ATT), 0)
    c = lax.broadcasted_iota(jnp.int32, (LANES, D_ATT), 1)
    col = ((c >> 6) & 1) * N_HEADS + (c >> 7)
    qt = jnp.where(r == col, jnp.broadcast_to(flat(q_ref), (LANES, D_ATT)), 0.0).astype(BF16)

    s = jnp.concatenate([_dot_nt(dense(kp), qt) for kp in k_pages], axis=0)
    s = s + bias_ref[...]
    s_new = _dot_nt(first_row(flat(kn_ref)), qt)
    s_new = jnp.where(lax.broadcasted_iota(jnp.int32, s_new.shape, 0) == 0, s_new, -jnp.inf)

    m = jnp.maximum(jnp.max(s, axis=0, keepdims=True), jnp.max(s_new, axis=0, keepdims=True))
    e = jnp.exp(s - m)
    e_new = jnp.exp(s_new - m)
    l = jnp.sum(e, axis=0, keepdims=True) + jnp.sum(e_new, axis=0, keepdims=True)
    p = (e / l)[:, 0:cols].astype(BF16)
    p_new = (e_new / l)[:, 0:cols].astype(BF16)

    acc = _dot_tn(p_new, first_row(flat(vn_ref)))
    for i, vp in enumerate(v_pages):
        acc = acc + _dot_tn(p[i * page:(i + 1) * page], dense(vp))

    lam = _diff_lambda(lam_ref, lam_init)
    for h in range(N_HEADS):
        o = (acc[h:h + 1, h * D_V:(h + 1) * D_V]
             - lam * acc[N_HEADS + h:N_HEADS + h + 1, h * D_V:(h + 1) * D_V])
        o_ref[:, h * D_V:(h + 1) * D_V] = (_rms(o, g_ref[...]) * (1.0 - lam_init)).astype(o_ref.dtype)


def _decode_attn_kernel(pt_ref, q_ref, kn_ref, vn_ref, bias_ref, lam_ref, g_ref, ck_hbm, cv_hbm,
                        o_ref, kbuf, vbuf, sem, *, layer, n_pages, lam_init):
    i = pl.program_id(0)
    n = pl.num_programs(0)

    def page_copies(seq, slot):
        out = []
        for j in range(n_pages):
            pg = pt_ref[seq, j]
            out.append(pltpu.make_async_copy(ck_hbm.at[layer, pg], kbuf.at[slot, j], sem.at[0, slot]))
            out.append(pltpu.make_async_copy(cv_hbm.at[layer, pg], vbuf.at[slot, j], sem.at[1, slot]))
        return out

    def start(seq, slot):
        for n_cp, cp in enumerate(page_copies(seq, slot)):
            cp.start(priority=(n_cp // 2) % 2)

    def wait(seq, slot):
        for cp in page_copies(seq, slot):
            cp.wait()

    def compute(slot):
        _decode_one(q_ref.at[slot], kn_ref.at[slot], vn_ref.at[slot], bias_ref, lam_ref, g_ref,
                    [kbuf.at[slot, j] for j in range(n_pages)],
                    [vbuf.at[slot, j] for j in range(n_pages)], o_ref.at[slot], lam_init)

    per_step = kbuf.shape[0]

    @pl.when(i == 0)
    def _():
        for s in range(DECODE_AHEAD):
            start(s, s)

    for k in range(per_step):
        nxt = k + DECODE_AHEAD
        if nxt < per_step:
            start(per_step * i + nxt, nxt)
        else:
            @pl.when(i + 1 < n)
            def _(nxt=nxt):
                start(per_step * i + nxt, nxt - per_step)
        wait(per_step * i + k, k)
        compute(k)


def _decode_attn_call(page_table, q4, k4, v4, bias, lamv, g_subln, cache_k, cache_v,
                      layer, lam_init, name):
    b = q4.shape[0]
    n_pages = page_table.shape[1]
    slab = cache_k.shape[2]
    per_step = DECODE_SLOTS
    tok = lambda i, pt: (i, 0, 0)
    const = lambda i, pt: (0, 0)
    grid_spec = pltpu.PrefetchScalarGridSpec(
        num_scalar_prefetch=1,
        grid=(b // per_step,),
        in_specs=[
            pl.BlockSpec((per_step, N_HEADS, D_V), tok),
            pl.BlockSpec((per_step, N_HEADS, D_V), tok),
            pl.BlockSpec((per_step, N_HEADS, D_V), tok),
            pl.BlockSpec(bias.shape, const),
            pl.BlockSpec((4, D_QK), const),
            pl.BlockSpec((1, D_V), const),
            pl.BlockSpec(memory_space=pl.ANY),
            pl.BlockSpec(memory_space=pl.ANY),
        ],
        out_specs=pl.BlockSpec((per_step, 1, D_ATT), tok),
        scratch_shapes=[
            pltpu.VMEM((per_step, n_pages, slab, D_V), F32),
            pltpu.VMEM((per_step, n_pages, slab, D_V), F32),
            pltpu.SemaphoreType.DMA((2, per_step)),
        ],
    )
    return pl.pallas_call(
        functools.partial(_decode_attn_kernel, layer=layer, n_pages=n_pages, lam_init=lam_init),
        grid_spec=grid_spec,
        out_shape=jax.ShapeDtypeStruct((b, 1, D_ATT), BF16),
        compiler_params=_cparams(1),
        name=name,
    )(page_table, q4, k4, v4, bias, lamv, g_subln, cache_k, cache_v)


def _lru_gates(u, wg_ref, bg_ref, lam_ref):
    gates = jax.nn.sigmoid(_dot(u.astype(BF16), wg_ref[...]) + bg_ref[...])
    r = gates[:, :D_LRU]
    i_g = gates[:, D_LRU:]
    neg_lam = -lam_ref[...]
    softplus = jnp.maximum(neg_lam, 0.0) + jnp.log1p(jnp.exp(-jnp.abs(neg_lam)))
    log_a = -LRU_C * r * softplus
    a = jnp.exp(log_a)
    one_minus_a2 = -jnp.tanh(log_a) * (a * a + 1.0)
    root = jnp.where(one_minus_a2 > 0.0, one_minus_a2 * lax.rsqrt(one_minus_a2), 0.0)
    return a, root * (i_g * u)


def _lru_seq_kernel(xr_ref, gr_ref, cw_ref, cb_ref, wg_ref, bg_ref, lam_ref,
                    y_ref, h_ref, tail_ref, xbuf, hbuf):
    t = T_LRU
    j = pl.program_id(1)

    @pl.when(j == 0)
    def _():
        xbuf[0:SUBLANES, :] = jnp.zeros((SUBLANES, D_LRU), F32)
        hbuf[...] = jnp.zeros(hbuf.shape, F32)

    x = xr_ref[...]
    xbuf[SUBLANES:SUBLANES + t, :] = x
    cw = cw_ref[...]
    u = cb_ref[...] + cw[CONV_WIDTH - 1:CONV_WIDTH] * x
    for d in range(1, CONV_WIDTH):
        u = u + cw[CONV_WIDTH - 1 - d:CONV_WIDTH - d] * xbuf[SUBLANES - d:SUBLANES - d + t, :]

    a, b = _lru_gates(u, wg_ref, bg_ref, lam_ref)

    groups = t // SUBLANES
    a = a.reshape(groups, SUBLANES, D_LRU)
    b = b.reshape(groups, SUBLANES, D_LRU)
    row = lax.broadcasted_iota(jnp.int32, a.shape, 1)
    shift = 1
    while shift < SUBLANES:
        a_prev = pltpu.roll(a, shift, 1)
        b_prev = pltpu.roll(b, shift, 1)
        live = row >= shift
        b = jnp.where(live, a * b_prev + b, b)
        a = jnp.where(live, a * a_prev, a)
        shift *= 2

    h = hbuf[SUBLANES - 1:SUBLANES, :]
    rows = []
    for g in range(groups):
        rows.append(b[g] + a[g] * h)
        h = rows[-1][SUBLANES - 1:SUBLANES, :]
    hs = jnp.concatenate(rows, axis=0)
    y_ref[...] = (hs * jax.nn.gelu(gr_ref[...])).astype(y_ref.dtype)

    hbuf[...] = hs[t - SUBLANES:t]
    xbuf[0:SUBLANES, :] = x[t - SUBLANES:t]
    h_ref[...] = hs[t - SUBLANES:t]
    tail_ref[...] = x[t - SUBLANES:t]


def _lru_seq_call(xr, gr, conv_w, conv_b, wg_bf, bg, lam, name):
    b, s, _ = xr.shape
    tok = lambda i, j: (i, j, 0)
    const = lambda i, j: (0, 0)
    per_seq = lambda i, j: (i, 0, 0)
    return pl.pallas_call(
        _lru_seq_kernel,
        grid=(b, s // T_LRU),
        in_specs=[
            pl.BlockSpec((None, T_LRU, D_LRU), tok),
            pl.BlockSpec((None, T_LRU, D_LRU), tok),
            pl.BlockSpec((CONV_WIDTH, D_LRU), const),
            pl.BlockSpec((1, D_LRU), const),
            pl.BlockSpec((D_LRU, 2 * D_LRU), const),
            pl.BlockSpec((1, 2 * D_LRU), const),
            pl.BlockSpec((1, D_LRU), const),
        ],
        out_specs=[
            pl.BlockSpec((None, T_LRU, D_LRU), tok),
            pl.BlockSpec((None, SUBLANES, D_LRU), per_seq),
            pl.BlockSpec((None, SUBLANES, D_LRU), per_seq),
        ],
        out_shape=[
            jax.ShapeDtypeStruct((b, s, D_LRU), BF16),
            jax.ShapeDtypeStruct((b, SUBLANES, D_LRU), F32),
            jax.ShapeDtypeStruct((b, SUBLANES, D_LRU), F32),
        ],
        scratch_shapes=[
            pltpu.VMEM((SUBLANES + T_LRU, D_LRU), F32),
            pltpu.VMEM((SUBLANES, D_LRU), F32),
        ],
        compiler_params=_cparams(2),
        name=name,
    )(xr, gr, conv_w, conv_b, wg_bf, bg, lam)


def _lru_step_kernel(xr_ref, gr_ref, cbuf_ref, h0_ref, cw_ref, cb_ref, wg_ref, bg_ref, lam_ref,
                     y_ref, h_ref):
    x = xr_ref[...]
    cw = cw_ref[...]
    u = cb_ref[...] + cw[CONV_WIDTH - 1:CONV_WIDTH] * x
    for d in range(CONV_WIDTH - 1):
        u = u + cw[d:d + 1] * cbuf_ref[d]
    a, b = _lru_gates(u, wg_ref, bg_ref, lam_ref)
    h = a * h0_ref[...] + b
    h_ref[...] = h
    y_ref[...] = (h * jax.nn.gelu(gr_ref[...])).astype(y_ref.dtype)


def _lru_step_call(xr, gr, cbuf_t, h0, conv_w, conv_b, wg_bf, bg, lam, name):
    b = xr.shape[0]
    return pl.pallas_call(
        _lru_step_kernel,
        out_shape=[jax.ShapeDtypeStruct((b, D_LRU), BF16),
                   jax.ShapeDtypeStruct((b, D_LRU), F32)],
        compiler_params=pltpu.CompilerParams(vmem_limit_bytes=VMEM_LIMIT_BYTES),
        name=name,
    )(xr, gr, cbuf_t, h0, conv_w, conv_b, wg_bf, bg, lam)


def _outffn_kernel(x_ref, oa_ref, yl_ref, gtm_ref, shf_ref, scf_ref, gtf_ref,
                   gpm_ref, gpf_ref, gqf_ref, wo_ref, wi_ref, wf_ref, o_ref):
    d_ff = wf_ref.shape[0]
    tm = x_ref.shape[0]
    n_groups = FFN_ROW_GROUPS if tm % (FFN_ROW_GROUPS * BF16_ROWS) == 0 else 1
    rows = [pl.ds(g * (tm // n_groups), tm // n_groups) for g in range(n_groups)]
    per_tok = lambda ref, r: ref[r, :] if ref.shape[0] == tm else ref[...]

    mix = [_dot(oa_ref[r, :], wo_ref[0:D_ATT, :]) + _dot(yl_ref[r, :], wo_ref[D_ATT:D_MODEL, :])
           for r in rows]
    x1 = [x_ref[r, :] + per_tok(gtm_ref, r) * _rms(m, gpm_ref[...]) for r, m in zip(rows, mix)]
    h = [(_rms(x, gpf_ref[...]) * (1.0 + per_tok(scf_ref, r)) + per_tok(shf_ref, r)).astype(BF16)
         for r, x in zip(rows, x1)]
    act = []
    for hg in h:
        gate = _dot(hg, wi_ref[:, 0:d_ff])
        up = _dot(hg, wi_ref[:, d_ff:2 * d_ff])
        act.append((gate * jax.nn.sigmoid(gate) * up).astype(BF16))
    f = [_dot(a, wf_ref[...]) for a in act]
    for r, x, fg in zip(rows, x1, f):
        o_ref[r, :] = x + per_tok(gtf_ref, r) * _rms(fg, gqf_ref[...])


def _outffn_call(x, oa, yl, gtm, shf, scf, gtf, gpm, gpf, gqf, wo_bf, wi_bf, wf_bf, tm, layer, name):
    b, s, _ = x.shape
    d_ff = wf_bf.shape[1]
    tok = lambda i, j: (i, j, 0)
    slab = lambda i, j: (layer, 0, 0)
    small = (gtm, shf, scf, gtf, gpm, gpf, gqf)
    resident = pl.Buffered(1)
    return pl.pallas_call(
        _outffn_kernel,
        grid=(b, s // tm),
        in_specs=[
            pl.BlockSpec((None, tm, D_MODEL), tok),
            pl.BlockSpec((None, tm, D_ATT), tok),
            pl.BlockSpec((None, tm, D_LRU), tok),
            *[spec for _, spec in small],
            pl.BlockSpec((None, D_MODEL, D_MODEL), slab, pipeline_mode=resident),
            pl.BlockSpec((None, D_MODEL, 2 * d_ff), slab, pipeline_mode=resident),
            pl.BlockSpec((None, d_ff, D_MODEL), slab, pipeline_mode=resident),
        ],
        out_specs=pl.BlockSpec((None, tm, D_MODEL), tok),
        out_shape=jax.ShapeDtypeStruct((b, s, D_MODEL), F32),
        compiler_params=_cparams(2),
        name=name,
    )(x, oa, yl, *[arr for arr, _ in small], wo_bf, wi_bf, wf_bf)


def _block_diag(w):
    n, i, j = w.shape
    eye = jnp.eye(n, dtype=w.dtype)
    return (eye[:, None, :, None] * w[:, :, None, :]).reshape(n * i, n * j)


def kernel(x_prompt, x_sample, cache_k, cache_v, state_h, state_conv, page_table, c_prompt, c_sample,
           w_ada, b_ada, g_pre_mix, w_in, lam_q1, lam_k1, lam_q2, lam_k2, g_subln, conv_w, conv_b,
           lru_wa, lru_ba, lru_wx, lru_bx, lru_lambda, w_out, g_post_mix, g_pre_ffn, w_ffn_in,
           w_ffn_out, g_post_ffn):
    depth = w_ada.shape[0]
    bp, seq, _ = x_prompt.shape
    bs = x_sample.shape[0]
    page = cache_k.shape[2]
    past = page_table.shape[1] * page

    pad = (-(bs + bp)) % SUBLANES
    c_all = jnp.concatenate([c_sample, c_prompt, jnp.zeros((pad, D_MODEL), F32)], axis=0)
    mod = _ada_call(c_all, w_ada, b_ada)
    mod_rows = mod.reshape(depth, mod.shape[1], 1, mod.shape[2])

    ys = x_sample.reshape(1, bs, D_MODEL)
    yp = x_prompt

    slope_of = lambda h: 2.0 ** (-8.0 * (h + 1) / N_HEADS)
    slopes = jnp.asarray([slope_of(h) for h in range(N_HEADS)], F32)
    dec_bias = np.zeros((past, LANES), np.float32)
    for j in range(2 * N_HEADS):
        dec_bias[:, j] = -slope_of(j % N_HEADS) * (past - np.arange(past, dtype=np.float32))
    dec_bias = jnp.asarray(dec_bias)
    ck = cache_k.reshape(depth, cache_k.shape[1], page * N_HEADS, D_V)
    cv = cache_v.reshape(depth, cache_v.shape[1], page * N_HEADS, D_V)

    w_in_bf, wo_bf, wi_bf, wf_bf = (w.astype(BF16) for w in (w_in, w_out, w_ffn_in, w_ffn_out))

    outs = {n: [] for n in ("hp", "cp", "ks", "vs", "hs", "cs")}
    kv_stack = ()
    for l in range(depth):
        lam_init = 0.8 - 0.6 * math.exp(-0.3 * l)
        lamv = jnp.stack([lam_q1[l], lam_k1[l], lam_q2[l], lam_k2[l]])
        g_sub = g_subln[l].reshape(1, D_V)
        wg_bf = jnp.concatenate([_block_diag(lru_wa[l]), _block_diag(lru_wx[l])], axis=1).astype(BF16)
        bg = jnp.concatenate([lru_ba[l].reshape(1, D_LRU), lru_bx[l].reshape(1, D_LRU)], axis=1)
        lam_row = lru_lambda[l].reshape(1, D_LRU)
        cb_row = conv_b[l].reshape(1, D_LRU)
        row = lambda v: (v.reshape(depth, 1, D_MODEL),
                         pl.BlockSpec((None, 1, D_MODEL), lambda i, j, l=l: (l, 0, 0)))
        mod_s = [(mod, pl.BlockSpec((None, bs, D_MODEL), lambda i, j, c=c, l=l: (l, j, c)))
                 for c in range(6)]
        mod_p = [(mod_rows, pl.BlockSpec((None, None, 1, D_MODEL), lambda i, j, c=c, l=l: (l, bs + i, 0, c)))
                 for c in range(6)]

        qt, kp_all, kb, vp_all, vt, xr, gr = _inproj_call(
            yp, mod_p[0], mod_p[1], row(g_pre_mix), w_in_bf, TM_PROJ, True, f"inproj_p{l}",
            layer=l, depth=depth, kv_stack=kv_stack)
        kv_stack = (kp_all, vp_all)
        oa = _attn_call(slopes, qt, kb, vt, lamv, g_sub, lam_init, f"attn_p{l}")
        yl, h8, x8 = _lru_seq_call(xr, gr, conv_w[l], cb_row, wg_bf, bg, lam_row, f"lru_p{l}")
        yp = _outffn_call(yp, oa, yl, mod_p[2], mod_p[3], mod_p[4], mod_p[5],
                          row(g_post_mix), row(g_pre_ffn), row(g_post_ffn),
                          wo_bf, wi_bf, wf_bf, TM_FFN, l, f"outffn_p{l}")
        outs["hp"].append(h8[:, SUBLANES - 1])
        outs["cp"].append(x8[:, SUBLANES - (CONV_WIDTH - 1):])

        q4, k4, v4, xr, gr = _inproj_call(ys, mod_s[0], mod_s[1], row(g_pre_mix), w_in_bf,
                                          bs, False, f"inproj_s{l}", layer=l)
        per_token = lambda a: a.reshape(bs, N_HEADS, D_V)
        oa = _decode_attn_call(page_table, per_token(q4), per_token(k4), per_token(v4), dec_bias, lamv, g_sub,
                               ck, cv, l, lam_init, f"attn_s{l}")
        yl, h_new = _lru_step_call(xr[0], gr[0], jnp.swapaxes(state_conv[l], 0, 1), state_h[l],
                                   conv_w[l], cb_row, wg_bf, bg, lam_row, f"lru_s{l}")
        ys = _outffn_call(ys, oa.reshape(1, bs, D_ATT), yl.reshape(1, bs, D_LRU),
                          mod_s[2], mod_s[3], mod_s[4], mod_s[5],
                          row(g_post_mix), row(g_pre_ffn), row(g_post_ffn),
                          wo_bf, wi_bf, wf_bf, bs, l, f"outffn_s{l}")
        outs["ks"].append(k4.reshape(bs, 1, N_HEADS, D_V))
        outs["vs"].append(v4.reshape(bs, 1, N_HEADS, D_V))
        outs["hs"].append(h_new)
        outs["cs"].append(jnp.concatenate([state_conv[l][:, 1:], xr[0][:, None, :]], axis=1))

    st = {n: jnp.stack(v) for n, v in outs.items()}
    kp_all, vp_all = (a.reshape(depth, bp, seq, N_HEADS, D_V) for a in kv_stack)
    return (yp, ys.reshape(bs, 1, D_MODEL), kp_all, vp_all, st["hp"], st["cp"],
            st["ks"], st["vs"], st["hs"], st["cs"])
```
